```python
import math
import jax, jax.numpy as jnp
from jax import lax
import numpy as np

D_MODEL = 2048
BATCH = 2
SEQ = 16384
DEPTH = 1

CONV_CH = 1024
CONV_KERNEL = 31
N_HEADS = 8
HEAD_DIM = 128
ATTN_WIDTH = N_HEADS * HEAD_DIM
ROT_DIM = HEAD_DIM // 4
ROPE_THETA = 500000.0
MOBA_BLOCK = 256
MOBA_TOPK = 3
Q_CHUNK = 64
N_BRANCH = 2
D_FF = 5632
FFN_CONV = 3
LN_EPS = 1e-5
DEEPNORM_ALPHA = (2.0 * DEPTH) ** 0.25
DEEPNORM_BETA = (8.0 * DEPTH) ** -0.25

IN_SPLITS = [2 * CONV_CH,
             2 * CONV_CH + ATTN_WIDTH,
             2 * CONV_CH + 2 * ATTN_WIDTH,
             2 * CONV_CH + 3 * ATTN_WIDTH]
IN_COLS = 2 * CONV_CH + 3 * ATTN_WIDTH + N_BRANCH * D_MODEL

kernel_name = "hybrid_conformer_moba_convffn_deepnorm_adaln"


def layer_norm(x, g, b):
    xf = x.astype(jnp.float32)
    mu = xf.mean(-1, keepdims=True)
    var = jnp.square(xf - mu).mean(-1, keepdims=True)
    return ((xf - mu) * lax.rsqrt(var + LN_EPS)).astype(x.dtype) * g + b


def causal_depthwise_conv(x, w, b):
    k = w.shape[0]
    xp = jnp.pad(x, ((0, 0), (k - 1, 0), (0, 0)))
    y = lax.conv_general_dilated(xp, w[:, None, :].astype(x.dtype), window_strides=(1,), padding="VALID",
                                 dimension_numbers=("NWC", "WIO", "NWC"),
                                 feature_group_count=x.shape[-1])
    return y + b


def rope_tables(seq_len, dtype):
    pos = jnp.arange(seq_len, dtype=jnp.float32)
    inv_freq = ROPE_THETA ** (-jnp.arange(0, ROT_DIM, 2, dtype=jnp.float32) / ROT_DIM)
    ang = pos[:, None] * inv_freq[None, :]
    return jnp.cos(ang).astype(dtype), jnp.sin(ang).astype(dtype)


def apply_partial_rope(x, cos, sin):
    half = ROT_DIM // 2
    x1, x2, rest = x[..., :half], x[..., half:ROT_DIM], x[..., ROT_DIM:]
    c = cos[None, :, None, :]
    s = sin[None, :, None, :]
    return jnp.concatenate([x1 * c - x2 * s, x1 * s + x2 * c, rest], axis=-1)


def moba_attention(q, k, v):
    b, h, s, dh = q.shape
    sp = -(-s // MOBA_BLOCK) * MOBA_BLOCK
    pad = ((0, 0), (0, 0), (0, sp - s), (0, 0))
    q, k, v = jnp.pad(q, pad), jnp.pad(k, pad), jnp.pad(v, pad)
    nb = sp // MOBA_BLOCK
    ks = min(MOBA_TOPK, nb)
    kb = k.reshape(b, h, nb, MOBA_BLOCK, dh)
    vb = v.reshape(b, h, nb, MOBA_BLOCK, dh)

    k_mean = kb.astype(jnp.float32).mean(axis=3)
    gate = jnp.einsum("bhsd,bhnd->bhsn", q.astype(jnp.float32), k_mean)
    q_block = jnp.arange(sp) // MOBA_BLOCK
    fully_past = jnp.arange(nb)[None, :] < q_block[:, None]
    gate = jnp.where(fully_past, gate, -jnp.inf)
    _, sel = lax.top_k(gate, ks)

    n_chunks = sp // Q_CHUNK
    q_c = q.reshape(b, h, n_chunks, Q_CHUNK, dh).transpose(2, 0, 1, 3, 4)
    sel_c = sel.reshape(b, h, n_chunks, Q_CHUNK, ks).transpose(2, 0, 1, 3, 4)
    gather_blocks = jax.vmap(jax.vmap(lambda blocks, idx: blocks[idx]))
    scale = HEAD_DIM ** -0.5

    def chunk_attend(args):
        ci, qc, selc = args
        q_start = ci * Q_CHUNK
        blk = q_start // MOBA_BLOCK
        q_pos = q_start + jnp.arange(Q_CHUNK)
        k_sel = gather_blocks(kb, selc)
        v_sel = gather_blocks(vb, selc)
        k_own = lax.dynamic_index_in_dim(kb, blk, axis=2, keepdims=False)
        v_own = lax.dynamic_index_in_dim(vb, blk, axis=2, keepdims=False)
        s_sel = jnp.einsum("bhqd,bhqntd->bhqnt", qc, k_sel,
                           preferred_element_type=jnp.float32).reshape(b, h, Q_CHUNK, ks * MOBA_BLOCK) * scale
        s_own = jnp.einsum("bhqd,bhtd->bhqt", qc, k_own, preferred_element_type=jnp.float32) * scale
        sel_ok = jnp.repeat(jnp.arange(ks) < blk, MOBA_BLOCK)
        own_ok = (blk * MOBA_BLOCK + jnp.arange(MOBA_BLOCK))[None, :] <= q_pos[:, None]
        logits = jnp.concatenate([jnp.where(sel_ok, s_sel, -jnp.inf),
                                  jnp.where(own_ok, s_own, -jnp.inf)], axis=-1)
        p = jax.nn.softmax(logits, axis=-1).astype(v.dtype)
        p_sel = p[..., :ks * MOBA_BLOCK].reshape(b, h, Q_CHUNK, ks, MOBA_BLOCK)
        p_own = p[..., ks * MOBA_BLOCK:]
        return (jnp.einsum("bhqnt,bhqntd->bhqd", p_sel, v_sel)
                + jnp.einsum("bhqt,bhtd->bhqd", p_own, v_own))

    out = lax.map(chunk_attend, (jnp.arange(n_chunks), q_c, sel_c))
    out = out.transpose(1, 2, 0, 3, 4).reshape(b, h, sp, dh)
    return out[:, :, :s]


def token_mixer(u, cos, sin, w_in, conv_dw_w, conv_dw_b, conv_ln_g, conv_ln_b,
                w_conv_out, b_conv_out, w_attn_out, w_out):
    b, s, _ = u.shape
    proj = u @ w_in
    glu_in, q, k, v, gate_logits = jnp.split(proj, IN_SPLITS, axis=-1)

    a_lin, a_gate = jnp.split(glu_in, 2, axis=-1)
    hc = a_lin * jax.nn.sigmoid(a_gate)
    hc = causal_depthwise_conv(hc, conv_dw_w, conv_dw_b)
    hc = jax.nn.silu(layer_norm(hc, conv_ln_g, conv_ln_b))
    y_conv = hc @ w_conv_out + b_conv_out

    q = apply_partial_rope(q.reshape(b, s, N_HEADS, HEAD_DIM), cos, sin).transpose(0, 2, 1, 3)
    k = apply_partial_rope(k.reshape(b, s, N_HEADS, HEAD_DIM), cos, sin).transpose(0, 2, 1, 3)
    v = v.reshape(b, s, N_HEADS, HEAD_DIM).transpose(0, 2, 1, 3)
    o = moba_attention(q, k, v).transpose(0, 2, 1, 3).reshape(b, s, ATTN_WIDTH)
    y_attn = o @ w_attn_out

    g_conv, g_attn = jnp.split(jax.nn.sigmoid(gate_logits), N_BRANCH, axis=-1)
    return (g_conv * y_conv + g_attn * y_attn) @ w_out


def conv_ffn(u, w_up, ffn_dw_w, ffn_dw_b, w_down):
    hid = u @ w_up
    a, val = jnp.split(hid, 2, axis=-1)
    a = causal_depthwise_conv(a, ffn_dw_w, ffn_dw_b)
    return (jax.nn.silu(a) * val) @ w_down


def setup_inputs(seed: int = 0) -> dict:
    key = jax.random.key(seed)
    ks = jax.random.split(key, 24)
    f32 = jnp.float32
    L = DEPTH

    def nrm(k, shape, scale):
        return jax.random.normal(k, shape, f32) * scale

    return {
        "x": nrm(ks[0], (BATCH, SEQ, D_MODEL), 1.0),
        "c": nrm(ks[1], (BATCH, D_MODEL), 1.0),
        "w_ada": nrm(ks[2], (L, D_MODEL, 6 * D_MODEL), 0.1 * D_MODEL ** -0.5),
        "b_ada": nrm(ks[3], (L, 6 * D_MODEL), 0.02),
        "w_in": nrm(ks[4], (L, D_MODEL, IN_COLS), D_MODEL ** -0.5),
        "conv_dw_w": nrm(ks[5], (L, CONV_KERNEL, CONV_CH), CONV_KERNEL ** -0.5),
        "conv_dw_b": nrm(ks[6], (L, CONV_CH), 0.02),
        "conv_ln_g": 1.0 + nrm(ks[7], (L, CONV_CH), 0.02),
        "conv_ln_b": nrm(ks[8], (L, CONV_CH), 0.02),
        "w_conv_out": nrm(ks[9], (L, CONV_CH, D_MODEL), CONV_CH ** -0.5),
        "b_conv_out": nrm(ks[10], (L, D_MODEL), 0.02),
        "w_attn_out": nrm(ks[11], (L, ATTN_WIDTH, D_MODEL), ATTN_WIDTH ** -0.5),
        "w_out": nrm(ks[12], (L, D_MODEL, D_MODEL), D_MODEL ** -0.5 * DEEPNORM_BETA),
        "ln1_g": 1.0 + nrm(ks[13], (L, D_MODEL), 0.02),
        "ln1_b": nrm(ks[14], (L, D_MODEL), 0.02),
        "w_up": nrm(ks[15], (L, D_MODEL, 2 * D_FF), D_MODEL ** -0.5),
        "ffn_dw_w": nrm(ks[16], (L, FFN_CONV, D_FF), FFN_CONV ** -0.5),
        "ffn_dw_b": nrm(ks[17], (L, D_FF), 0.02),
        "w_down": nrm(ks[18], (L, D_FF, D_MODEL), D_FF ** -0.5 * DEEPNORM_BETA),
        "ln2_g": 1.0 + nrm(ks[19], (L, D_MODEL), 0.02),
        "ln2_b": nrm(ks[20], (L, D_MODEL), 0.02),
    }


def reference(x, c, w_ada, b_ada, w_in, conv_dw_w, conv_dw_b, conv_ln_g, conv_ln_b,
              w_conv_out, b_conv_out, w_attn_out, w_out, ln1_g, ln1_b,
              w_up, ffn_dw_w, ffn_dw_b, w_down, ln2_g, ln2_b):
    cos, sin = rope_tables(x.shape[1], x.dtype)
    c_act = jax.nn.silu(c)
    for l in range(DEPTH):
        mod = c_act @ w_ada[l] + b_ada[l]
        shift1, scale1, gate1, shift2, scale2, gate2 = jnp.split(mod[:, None, :], 6, axis=-1)
        u = x * (1 + scale1) + shift1
        y = token_mixer(u, cos, sin, w_in[l], conv_dw_w[l], conv_dw_b[l], conv_ln_g[l], conv_ln_b[l],
                        w_conv_out[l], b_conv_out[l], w_attn_out[l], w_out[l])
        x = layer_norm(DEEPNORM_ALPHA * x + (1 + gate1) * y, ln1_g[l], ln1_b[l])
        u = x * (1 + scale2) + shift2
        y = conv_ffn(u, w_up[l], ffn_dw_w[l], ffn_dw_b[l], w_down[l])
        x = layer_norm(DEEPNORM_ALPHA * x + (1 + gate2) * y, ln2_g[l], ln2_b[l])
    return x
```

```python
import functools

import jax
import jax.numpy as jnp
from jax import lax
from jax.experimental import pallas as pl
from jax.experimental.pallas import tpu as pltpu

F32 = jnp.float32
BF16 = jnp.bfloat16

N_HEADS = 8
HEAD_DIM = 128
ROT_DIM = HEAD_DIM // 4
ROPE_THETA = 500000.0
MOBA_BLOCK = 256
MOBA_TOPK = 3
CONV_KERNEL = 31
FFN_CONV = 3
LN_EPS = 1e-5
DEPTH = 1
DEEPNORM_ALPHA = (2.0 * DEPTH) ** 0.25

LANES = 128
SUBLANES = 8
MASK_BIAS = -(2.0 ** 30)
CONV_HALO = 32
FFN_HALO = SUBLANES
VMEM_LIMIT = 56 * 1024 * 1024


def _params(*sem):
    return pltpu.CompilerParams(dimension_semantics=sem, vmem_limit_bytes=VMEM_LIMIT)


def _const_spec(shape):
    nd = len(shape)
    return pl.BlockSpec(shape, lambda *_: (0,) * nd, pipeline_mode=pl.Buffered(1))


def _layer_norm(z, g, b):
    mu = jnp.mean(z, axis=-1, keepdims=True)
    zc = z - mu
    var = jnp.mean(zc * zc, axis=-1, keepdims=True)
    return zc * lax.rsqrt(var + LN_EPS) * g + b


def _silu(v):
    return v * jax.nn.sigmoid(v)


def _ada_kernel(c_ref, w_ref, b_ref, o_ref):
    ca = _silu(c_ref[...]).astype(BF16)
    o_ref[...] = jnp.dot(ca, w_ref[...].astype(BF16), preferred_element_type=F32) + b_ref[...]


def _ada(c_pad, w_ada, b_ada, tn=1024):
    rows, d = c_pad.shape
    n = w_ada.shape[1]
    return pl.pallas_call(
        _ada_kernel,
        grid=(n // tn,),
        in_specs=[pl.BlockSpec((rows, d), lambda j: (0, 0)),
                  pl.BlockSpec((d, tn), lambda j: (0, j)),
                  pl.BlockSpec((1, tn), lambda j: (0, j))],
        out_specs=pl.BlockSpec((rows, tn), lambda j: (0, j)),
        out_shape=jax.ShapeDtypeStruct((rows, n), F32),
        compiler_params=_params("arbitrary"),
        name="ada",
    )(c_pad, w_ada, b_ada)


def _inproj_kernel(x_ref, sc_ref, sh_ref, w_ref, o_ref, u_ref):
    @pl.when(pl.program_id(1) == 0)
    def _():
        u_ref[...] = (x_ref[...] * (1.0 + sc_ref[0]) + sh_ref[0]).astype(BF16)

    o_ref[...] = jnp.dot(u_ref[...], w_ref[...], preferred_element_type=F32)


def _inproj(x2d, scale, shift, w, seq, tm, tn):
    t, d = x2d.shape
    n = w.shape[1]
    tpb = seq // tm
    return pl.pallas_call(
        _inproj_kernel,
        grid=(t // tm, n // tn),
        in_specs=[pl.BlockSpec((tm, d), lambda i, j: (i, 0)),
                  pl.BlockSpec((1, 1, d), lambda i, j: (i // tpb, 0, 0)),
                  pl.BlockSpec((1, 1, d), lambda i, j: (i // tpb, 0, 0)),
                  pl.BlockSpec((d, tn), lambda i, j: (0, j))],
        out_specs=pl.BlockSpec((tm, tn), lambda i, j: (i, j)),
        out_shape=jax.ShapeDtypeStruct((t, n), F32),
        scratch_shapes=[pltpu.VMEM((tm, d), BF16)],
        compiler_params=_params("arbitrary", "arbitrary"),
        name="inproj",
    )(x2d, scale, shift, w)


def _conv_kernel(al_ref, ag_ref, w_ref, b_ref, g_ref, be_ref, o_ref, win_ref, *, tm, sub):
    i = pl.program_id(1)

    ch = win_ref.shape[2]

    @pl.when(i == 0)
    def _():
        win_ref[0, 0:CONV_HALO, :] = jnp.zeros((CONV_HALO, ch), F32)

    @pl.when(i > 0)
    def _():
        win_ref[0, 0:CONV_HALO, :] = win_ref[0, tm:tm + CONV_HALO, :]

    win_ref[0, CONV_HALO:CONV_HALO + tm, :] = al_ref[...] * jax.nn.sigmoid(ag_ref[...])

    n_shift = tm + CONV_HALO - SUBLANES
    for s in range(1, SUBLANES):
        win_ref[s, 0:n_shift, :] = win_ref[0, s:s + n_shift, :]

    first = CONV_HALO - (CONV_KERNEL - 1)

    def body(s, carry):
        r0 = pl.multiple_of(s * sub, sub)
        acc = jnp.zeros((sub, ch), F32) + b_ref[...]
        for j in range(CONV_KERNEL):
            tiles, rem = divmod(first + j, SUBLANES)
            start = pl.multiple_of(r0 + tiles * SUBLANES, SUBLANES)
            acc = acc + win_ref[rem, pl.ds(start, sub), :] * w_ref[j:j + 1, :]
        y = _silu(_layer_norm(acc, g_ref[...], be_ref[...]))
        o_ref[pl.ds(r0, sub), :] = y.astype(o_ref.dtype)
        return carry

    lax.fori_loop(0, tm // sub, body, 0)


def _conv_branch(proj, dw_w, dw_b, ln_g, ln_b, batch, seq, ch, tm=512, sub=16):
    t = proj.shape[0]
    tpb = seq // tm
    kern = functools.partial(_conv_kernel, tm=tm, sub=sub)
    return pl.pallas_call(
        kern,
        grid=(batch, tpb),
        in_specs=[pl.BlockSpec((tm, ch), lambda b, i: (b * tpb + i, 0)),
                  pl.BlockSpec((tm, ch), lambda b, i: (b * tpb + i, 1)),
                  _const_spec(dw_w.shape), _const_spec((1, ch)), _const_spec((1, ch)), _const_spec((1, ch))],
        out_specs=pl.BlockSpec((tm, ch), lambda b, i: (b * tpb + i, 0)),
        out_shape=jax.ShapeDtypeStruct((t, ch), BF16),
        scratch_shapes=[pltpu.VMEM((SUBLANES, tm + CONV_HALO, ch), F32)],
        compiler_params=_params("arbitrary", "arbitrary"),
        name="conv_branch",
    )(proj, proj, dw_w, dw_b, ln_g, ln_b)


def _prep_kernel(q_ref, k_ref, v_ref, c_ref, a_ref, b_ref, qo_ref, ko_ref, vo_ref, km_ref):
    cc, aa, bb = c_ref[...], a_ref[...], b_ref[...]

    def rope(xh):
        return xh * cc + pltpu.roll(xh, LANES - ROT_DIM // 2, 1) * aa + pltpu.roll(xh, ROT_DIM // 2, 1) * bb

    for h in range(N_HEADS):
        sl = slice(h * HEAD_DIM, (h + 1) * HEAD_DIM)
        qo_ref[:, sl] = rope(q_ref[:, sl]).astype(BF16)
        kr = rope(k_ref[:, sl])
        ko_ref[:, sl] = kr.astype(BF16)
        km_ref[0, :, sl] = jnp.mean(kr, axis=0, keepdims=True)
    vo_ref[...] = v_ref[...].astype(BF16)


def _prep(proj, rope_c, rope_a, rope_b, batch, seq, qcol):
    t = proj.shape[0]
    w = N_HEADS * HEAD_DIM
    nb = seq // MOBA_BLOCK
    blk = lambda c: pl.BlockSpec((MOBA_BLOCK, w), lambda b, n: (b * nb + n, c))
    tab = pl.BlockSpec((MOBA_BLOCK, HEAD_DIM), lambda b, n: (n, 0))
    out = pl.BlockSpec((MOBA_BLOCK, w), lambda b, n: (b * nb + n, 0))
    return pl.pallas_call(
        _prep_kernel,
        grid=(batch, nb),
        in_specs=[blk(qcol), blk(qcol + 1), blk(qcol + 2), tab, tab, tab],
        out_specs=[out, out, out, pl.BlockSpec((1, 1, w), lambda b, n: (b * nb + n, 0, 0))],
        out_shape=[jax.ShapeDtypeStruct((t, w), BF16)] * 3 + [jax.ShapeDtypeStruct((batch * nb, 1, w), F32)],
        compiler_params=_params("arbitrary", "arbitrary"),
        name="rope_prep",
    )(proj, proj, proj, rope_c, rope_a, rope_b)


def _attn_kernel(q_ref, k_ref, v_ref, km_ref, o_ref):
    qb = pl.program_id(2)
    blk = MOBA_BLOCK
    scale = HEAD_DIM ** -0.5
    q = q_ref[...]

    km = jnp.concatenate([km_ref[0].astype(BF16), jnp.zeros((LANES - km_ref.shape[1], HEAD_DIM), BF16)], axis=0)
    gate = lax.dot_general(q, km, (((1,), (1,)), ((), ())), preferred_element_type=F32)
    ids = lax.broadcasted_iota(jnp.int32, gate.shape, 1).astype(F32)
    n_past = jnp.full(gate.shape, qb.astype(F32))
    g = jnp.where(ids < n_past, gate, -jnp.inf)
    sel = jnp.zeros(gate.shape, jnp.bool_)
    for r in range(MOBA_TOPK):
        mx = jnp.max(g, axis=-1, keepdims=True)
        first = jnp.min(jnp.where(g == mx, ids, float(LANES)), axis=-1, keepdims=True)
        pick = ids == first
        sel = jnp.logical_or(sel, jnp.logical_and(pick, n_past > float(r)))
        g = jnp.where(pick, -jnp.inf, g)
    bias = jnp.where(sel, 0.0, MASK_BIAS).astype(BF16)
    q_aug = jnp.concatenate([q, bias], axis=1)

    start = pl.multiple_of(qb * blk, blk)
    k_own = k_ref[pl.ds(start, blk), :]
    v_own = v_ref[pl.ds(start, blk), :]
    s = lax.dot_general(q, k_own, (((1,), (1,)), ((), ())), preferred_element_type=F32) * scale
    row = lax.broadcasted_iota(jnp.int32, s.shape, 0)
    col = lax.broadcasted_iota(jnp.int32, s.shape, 1)
    s = jnp.where(col <= row, s, -jnp.inf)
    m0 = jnp.max(s, axis=-1, keepdims=True)
    p = jnp.exp(s - m0)
    l0 = jnp.sum(p, axis=-1, keepdims=True)
    acc0 = jnp.dot(p.astype(BF16), v_own, preferred_element_type=F32)

    lane = lax.broadcasted_iota(jnp.int32, (blk, LANES), 1)

    def body(j, carry):
        m, l, acc = carry
        off = pl.multiple_of(j * blk, blk)
        kj = k_ref[pl.ds(off, blk), :]
        vj = v_ref[pl.ds(off, blk), :]
        k_aug = jnp.concatenate([kj, jnp.where(lane == j, 1.0, 0.0).astype(BF16)], axis=1)
        sj = lax.dot_general(q_aug, k_aug, (((1,), (1,)), ((), ())), preferred_element_type=F32) * scale
        m_new = jnp.maximum(m, jnp.max(sj, axis=-1, keepdims=True))
        pj = jnp.exp(sj - m_new)
        corr = jnp.exp(m - m_new)
        l_new = corr * l + jnp.sum(pj, axis=-1, keepdims=True)
        acc_new = corr * acc + jnp.dot(pj.astype(BF16), vj, preferred_element_type=F32)
        return m_new, l_new, acc_new

    m, l, acc = lax.fori_loop(0, qb, body, (m0, l0, acc0))
    o_ref[...] = (acc / l).astype(o_ref.dtype)


def _attention(qr, kr, vb, kmean, batch, seq):
    t, w = qr.shape
    nb = seq // MOBA_BLOCK
    return pl.pallas_call(
        _attn_kernel,
        grid=(batch, N_HEADS, nb),
        in_specs=[pl.BlockSpec((MOBA_BLOCK, HEAD_DIM), lambda b, h, n: (b * nb + n, h)),
                  pl.BlockSpec((seq, HEAD_DIM), lambda b, h, n: (b, h)),
                  pl.BlockSpec((seq, HEAD_DIM), lambda b, h, n: (b, h)),
                  pl.BlockSpec((1, nb, HEAD_DIM), lambda b, h, n: (b, 0, h))],
        out_specs=pl.BlockSpec((MOBA_BLOCK, HEAD_DIM), lambda b, h, n: (b * nb + n, h)),
        out_shape=jax.ShapeDtypeStruct((t, w), BF16),
        compiler_params=_params("arbitrary", "arbitrary", "arbitrary"),
        name="moba_attn",
    )(qr, kr, vb, kmean)


def _merge_kernel(hc_ref, o_ref, gc0_ref, gc1_ref, ga0_ref, ga1_ref, x_ref, g1_ref, sc2_ref, sh2_ref,
                  wc_ref, bc_ref, wa_ref, wo_ref, lg_ref, lb_ref, x1_ref, u2_ref):
    yc = jnp.dot(hc_ref[...], wc_ref[...], preferred_element_type=F32) + bc_ref[...]
    ya = jnp.dot(o_ref[...], wa_ref[...], preferred_element_type=F32)
    gc = jax.nn.sigmoid(jnp.concatenate([gc0_ref[...], gc1_ref[...]], axis=1))
    ga = jax.nn.sigmoid(jnp.concatenate([ga0_ref[...], ga1_ref[...]], axis=1))
    mix = (gc * yc + ga * ya).astype(BF16)
    y = jnp.dot(mix, wo_ref[...], preferred_element_type=F32)
    z = DEEPNORM_ALPHA * x_ref[...] + (1.0 + g1_ref[0]) * y
    x1 = _layer_norm(z, lg_ref[...], lb_ref[...])
    x1_ref[...] = x1
    u2_ref[...] = (x1 * (1.0 + sc2_ref[0]) + sh2_ref[0]).astype(BF16)


def _merge(hc, o, proj, x2d, gate1, scale2, shift2, wc, bc, wa, wo, lg, lb, seq, gcol, tm=256):
    t, d = x2d.shape
    ch = hc.shape[1]
    w = o.shape[1]
    tpb = seq // tm
    row = lambda cols, c: pl.BlockSpec((tm, cols), lambda i: (i, c))
    mod = pl.BlockSpec((1, 1, d), lambda i: (i // tpb, 0, 0))
    return pl.pallas_call(
        _merge_kernel,
        grid=(t // tm,),
        in_specs=[row(ch, 0), row(w, 0),
                  row(d // 2, gcol), row(d // 2, gcol + 1), row(d // 2, gcol + 2), row(d // 2, gcol + 3),
                  row(d, 0), mod, mod, mod,
                  _const_spec(wc.shape), _const_spec((1, d)), _const_spec(wa.shape), _const_spec(wo.shape),
                  _const_spec((1, d)), _const_spec((1, d))],
        out_specs=[row(d, 0), row(d, 0)],
        out_shape=[jax.ShapeDtypeStruct((t, d), F32), jax.ShapeDtypeStruct((t, d), BF16)],
        compiler_params=_params("arbitrary"),
        name="merge_ln1",
    )(hc, o, proj, proj, proj, proj, x2d, gate1, scale2, shift2, wc, bc, wa, wo, lg, lb)


def _ffn_kernel(u_ref, wa_ref, wv_ref, cw_ref, cb_ref, wd_ref, x1_ref, g2_ref, lg_ref, lb_ref, o_ref,
                acc_ref, abuf_ref, carry_ref, *, tm, tpb):
    i = pl.program_id(0)
    f = pl.program_id(1)
    nf = pl.num_programs(1)
    u = u_ref[...]
    a = jnp.dot(u, wa_ref[...], preferred_element_type=F32)
    val = jnp.dot(u, wv_ref[...], preferred_element_type=F32)

    first_of_batch = (i % tpb) == 0

    @pl.when(first_of_batch)
    def _():
        abuf_ref[0:FFN_HALO, :] = jnp.zeros((FFN_HALO, abuf_ref.shape[1]), F32)

    @pl.when(jnp.logical_not(first_of_batch))
    def _():
        abuf_ref[0:FFN_HALO, :] = carry_ref[f]

    abuf_ref[FFN_HALO:FFN_HALO + tm, :] = a
    carry_ref[f] = a[tm - FFN_HALO:, :]
    conv = a * cw_ref[FFN_CONV - 1:FFN_CONV, :] + cb_ref[...]
    for j in range(FFN_CONV - 1):
        shift = FFN_CONV - 1 - j
        conv = conv + abuf_ref[FFN_HALO - shift:FFN_HALO - shift + tm, :] * cw_ref[j:j + 1, :]
    hid = (_silu(conv) * val).astype(BF16)
    part = jnp.dot(hid, wd_ref[...], preferred_element_type=F32)

    @pl.when(f == 0)
    def _():
        acc_ref[...] = part

    @pl.when(f > 0)
    def _():
        acc_ref[...] += part

    @pl.when(f == nf - 1)
    def _():
        z = DEEPNORM_ALPHA * x1_ref[...] + (1.0 + g2_ref[0]) * acc_ref[...]
        o_ref[...] = _layer_norm(z, lg_ref[...], lb_ref[...])


def _ffn(u2, w_up, cw, cb, w_down, x1, gate2, lg, lb, seq, tm=512, tf=512):
    t, d = u2.shape
    dff = w_down.shape[0]
    nf = dff // tf
    tpb = seq // tm
    kern = functools.partial(_ffn_kernel, tm=tm, tpb=tpb)
    return pl.pallas_call(
        kern,
        grid=(t // tm, nf),
        in_specs=[pl.BlockSpec((tm, d), lambda i, f: (i, 0)),
                  pl.BlockSpec((d, tf), lambda i, f: (0, f)),
                  pl.BlockSpec((d, tf), lambda i, f: (0, nf + f)),
                  pl.BlockSpec((FFN_CONV, tf), lambda i, f: (0, f)),
                  pl.BlockSpec((1, tf), lambda i, f: (0, f)),
                  pl.BlockSpec((tf, d), lambda i, f: (f, 0)),
                  pl.BlockSpec((tm, d), lambda i, f: (i, 0)),
                  pl.BlockSpec((1, 1, d), lambda i, f: (i // tpb, 0, 0)),
                  pl.BlockSpec((1, d), lambda i, f: (0, 0)),
                  pl.BlockSpec((1, d), lambda i, f: (0, 0))],
        out_specs=pl.BlockSpec((tm, d), lambda i, f: (i, 0)),
        out_shape=jax.ShapeDtypeStruct((t, d), F32),
        scratch_shapes=[pltpu.VMEM((tm, d), F32),
                        pltpu.VMEM((tm + FFN_HALO, tf), F32),
                        pltpu.VMEM((nf, FFN_HALO, tf), F32)],
        compiler_params=_params("arbitrary", "arbitrary"),
        name="convffn_ln2",
    )(u2, w_up, w_up, cw, cb, w_down, x1, gate2, lg, lb)


def _rope_tables(seq):
    half = ROT_DIM // 2
    pos = jnp.arange(seq, dtype=F32)
    inv_freq = ROPE_THETA ** (-jnp.arange(0, ROT_DIM, 2, dtype=F32) / ROT_DIM)
    ang = pos[:, None] * inv_freq[None, :]
    cos, sin = jnp.cos(ang), jnp.sin(ang)
    ones = jnp.ones((seq, HEAD_DIM - ROT_DIM), F32)
    zeros = jnp.zeros((seq, HEAD_DIM - ROT_DIM), F32)
    zh = jnp.zeros((seq, half), F32)
    tab_c = jnp.concatenate([cos, cos, ones], axis=1)
    tab_a = jnp.concatenate([-sin, zh, zeros], axis=1)
    tab_b = jnp.concatenate([zh, sin, zeros], axis=1)
    return tab_c, tab_a, tab_b


def kernel(x, c, w_ada, b_ada, w_in, conv_dw_w, conv_dw_b, conv_ln_g, conv_ln_b, w_conv_out, b_conv_out,
           w_attn_out, w_out, ln1_g, ln1_b, w_up, ffn_dw_w, ffn_dw_b, w_down, ln2_g, ln2_b):
    batch, seq, d = x.shape
    ch = conv_dw_w.shape[-1]
    width = N_HEADS * HEAD_DIM
    assert seq % MOBA_BLOCK == 0 and w_ada.shape[0] == DEPTH
    assert 2 * ch % width == 0 and (2 * ch + 3 * width) % (d // 2) == 0
    x2d = x.reshape(batch * seq, d)
    row = lambda v: v.reshape(1, -1)

    c_pad = jnp.pad(c, ((0, SUBLANES - batch), (0, 0)))
    mod = _ada(c_pad, w_ada[0], row(b_ada[0]))[:batch]
    shift1, scale1, gate1, shift2, scale2, gate2 = [m.reshape(batch, 1, d) for m in jnp.split(mod, 6, axis=-1)]

    proj = _inproj(x2d, scale1, shift1, w_in[0].astype(BF16), seq, tm=min(1024, seq), tn=1024)

    hc = _conv_branch(proj, conv_dw_w[0], row(conv_dw_b[0]), row(conv_ln_g[0]), row(conv_ln_b[0]),
                      batch, seq, ch, tm=min(256, seq))

    tab_c, tab_a, tab_b = _rope_tables(seq)
    qr, kr, vb, kmean = _prep(proj, tab_c, tab_a, tab_b, batch, seq, qcol=2 * ch // width)
    kmean = kmean.reshape(batch, seq // MOBA_BLOCK, width)
    o = _attention(qr, kr, vb, kmean, batch, seq)

    x1, u2 = _merge(hc, o, proj, x2d, gate1, scale2, shift2,
                    w_conv_out[0].astype(BF16), row(b_conv_out[0]), w_attn_out[0].astype(BF16),
                    w_out[0].astype(BF16), row(ln1_g[0]), row(ln1_b[0]), seq,
                    gcol=(2 * ch + 3 * width) // (d // 2))

    out = _ffn(u2, w_up[0].astype(BF16), ffn_dw_w[0], row(ffn_dw_b[0]), w_down[0].astype(BF16),
               x1, gate2, row(ln2_g[0]), row(ln2_b[0]), seq, tm=min(512, seq))
    return out.reshape(batch, seq, d)
```

```python
import functools

import jax
import jax.numpy as jnp
from jax import lax
from jax.experimental import pallas as pl
from jax.experimental.pallas import tpu as pltpu

F32 = jnp.float32
BF16 = jnp.bfloat16

N_HEADS = 8
HEAD_DIM = 128
ROT_DIM = HEAD_DIM // 4
ROPE_THETA = 500000.0
MOBA_BLOCK = 256
MOBA_TOPK = 3
CONV_KERNEL = 31
FFN_CONV = 3
LN_EPS = 1e-5
DEPTH = 1
DEEPNORM_ALPHA = (2.0 * DEPTH) ** 0.25

LANES = 128
SUBLANES = 8
MASK_BIAS = -(2.0 ** 30)
LOG2_E = 1.4426950408889634
CONV_HALO = 32
FFN_HALO = SUBLANES
VMEM_LIMIT = 56 * 1024 * 1024


def _params(*sem):
    return pltpu.CompilerParams(dimension_semantics=sem, vmem_limit_bytes=VMEM_LIMIT)


def _const_spec(shape):
    nd = len(shape)
    return pl.BlockSpec(shape, lambda *_: (0,) * nd, pipeline_mode=pl.Buffered(1))


def _layer_norm(z, g, b):
    mu = jnp.mean(z, axis=-1, keepdims=True)
    zc = z - mu
    var = jnp.mean(zc * zc, axis=-1, keepdims=True)
    return zc * lax.rsqrt(var + LN_EPS) * g + b


def _silu(v):
    return v * jax.nn.sigmoid(v)


def _ada_kernel(c_ref, w_ref, b_ref, o_ref):
    ca = _silu(c_ref[...]).astype(BF16)
    o_ref[...] = jnp.dot(ca, w_ref[...].astype(BF16), preferred_element_type=F32) + b_ref[...]


def _ada(c_pad, w_ada, b_ada, tn=1024):
    rows, d = c_pad.shape
    n = w_ada.shape[1]
    return pl.pallas_call(
        _ada_kernel,
        grid=(n // tn,),
        in_specs=[pl.BlockSpec((rows, d), lambda j: (0, 0)),
                  pl.BlockSpec((d, tn), lambda j: (0, j)),
                  pl.BlockSpec((1, tn), lambda j: (0, j))],
        out_specs=pl.BlockSpec((rows, tn), lambda j: (0, j)),
        out_shape=jax.ShapeDtypeStruct((rows, n), F32),
        compiler_params=_params("arbitrary"),
        name="ada",
    )(c_pad, w_ada, b_ada)


def _inproj_kernel(x_ref, sc_ref, sh_ref, w_ref, o_ref, u_ref):
    @pl.when(pl.program_id(1) == 0)
    def _():
        u_ref[...] = (x_ref[...] * (1.0 + sc_ref[0]) + sh_ref[0]).astype(BF16)

    o_ref[...] = jnp.dot(u_ref[...], w_ref[...], preferred_element_type=F32)


def _inproj(x2d, scale, shift, w, seq, tm, tn):
    t, d = x2d.shape
    n = w.shape[1]
    tpb = seq // tm
    return pl.pallas_call(
        _inproj_kernel,
        grid=(t // tm, n // tn),
        in_specs=[pl.BlockSpec((tm, d), lambda i, j: (i, 0)),
                  pl.BlockSpec((1, 1, d), lambda i, j: (i // tpb, 0, 0)),
                  pl.BlockSpec((1, 1, d), lambda i, j: (i // tpb, 0, 0)),
                  pl.BlockSpec((d, tn), lambda i, j: (0, j))],
        out_specs=pl.BlockSpec((tm, tn), lambda i, j: (i, j)),
        out_shape=jax.ShapeDtypeStruct((t, n), F32),
        scratch_shapes=[pltpu.VMEM((tm, d), BF16)],
        compiler_params=_params("arbitrary", "arbitrary"),
        name="inproj",
    )(x2d, scale, shift, w)


def _conv_kernel(al_ref, ag_ref, w_ref, b_ref, g_ref, be_ref, o_ref, win_ref, *, tm, sub):
    i = pl.program_id(1)

    ch = win_ref.shape[2]

    @pl.when(i == 0)
    def _():
        win_ref[0, 0:CONV_HALO, :] = jnp.zeros((CONV_HALO, ch), F32)

    @pl.when(i > 0)
    def _():
        win_ref[0, 0:CONV_HALO, :] = win_ref[0, tm:tm + CONV_HALO, :]

    win_ref[0, CONV_HALO:CONV_HALO + tm, :] = al_ref[...] * jax.nn.sigmoid(ag_ref[...])

    n_shift = tm + CONV_HALO - SUBLANES
    for s in range(1, SUBLANES):
        win_ref[s, 0:n_shift, :] = win_ref[0, s:s + n_shift, :]

    first = CONV_HALO - (CONV_KERNEL - 1)

    def body(s, carry):
        r0 = pl.multiple_of(s * sub, sub)
        acc = jnp.zeros((sub, ch), F32) + b_ref[...]
        for j in range(CONV_KERNEL):
            tiles, rem = divmod(first + j, SUBLANES)
            start = pl.multiple_of(r0 + tiles * SUBLANES, SUBLANES)
            acc = acc + win_ref[rem, pl.ds(start, sub), :] * w_ref[j:j + 1, :]
        y = _silu(_layer_norm(acc, g_ref[...], be_ref[...]))
        o_ref[pl.ds(r0, sub), :] = y.astype(o_ref.dtype)
        return carry

    lax.fori_loop(0, tm // sub, body, 0)


def _conv_branch(proj, dw_w, dw_b, ln_g, ln_b, batch, seq, ch, tm=512, sub=16):
    t = proj.shape[0]
    tpb = seq // tm
    kern = functools.partial(_conv_kernel, tm=tm, sub=sub)
    return pl.pallas_call(
        kern,
        grid=(batch, tpb),
        in_specs=[pl.BlockSpec((tm, ch), lambda b, i: (b * tpb + i, 0)),
                  pl.BlockSpec((tm, ch), lambda b, i: (b * tpb + i, 1)),
                  _const_spec(dw_w.shape), _const_spec((1, ch)), _const_spec((1, ch)), _const_spec((1, ch))],
        out_specs=pl.BlockSpec((tm, ch), lambda b, i: (b * tpb + i, 0)),
        out_shape=jax.ShapeDtypeStruct((t, ch), BF16),
        scratch_shapes=[pltpu.VMEM((SUBLANES, tm + CONV_HALO, ch), F32)],
        compiler_params=_params("arbitrary", "arbitrary"),
        name="conv_branch",
    )(proj, proj, dw_w, dw_b, ln_g, ln_b)


def _prep_kernel(q_ref, k_ref, v_ref, c_ref, a_ref, b_ref, qa_ref, ka_ref, vt_ref, km_ref):
    n = pl.program_id(1)
    blk = q_ref.shape[0]

    @pl.when(n == 0)
    def _():
        km_ref[...] = jnp.zeros(km_ref.shape, F32)

    cc, aa, bb = c_ref[...], a_ref[...], b_ref[...]

    def rope(xh):
        return xh * cc + pltpu.roll(xh, LANES - ROT_DIM // 2, 1) * aa + pltpu.roll(xh, ROT_DIM // 2, 1) * bb

    lane = lax.broadcasted_iota(jnp.int32, (blk, LANES), 1)
    ids = lane.astype(F32)
    n_past = jnp.full((blk, LANES), n.astype(F32))
    onehot_n = jnp.where(lane == n, 1.0, 0.0).astype(BF16)
    km_row = lax.broadcasted_iota(jnp.int32, (km_ref.shape[0], HEAD_DIM), 0)
    scale = HEAD_DIM ** -0.5 * LOG2_E

    for h in range(N_HEADS):
        sl = slice(h * HEAD_DIM, (h + 1) * HEAD_DIM)
        lo = slice(2 * h * HEAD_DIM, (2 * h + 1) * HEAD_DIM)
        hi = slice((2 * h + 1) * HEAD_DIM, (2 * h + 2) * HEAD_DIM)
        qr = rope(q_ref[:, sl])
        kr = rope(k_ref[:, sl])
        km = km_ref[:, sl]

        gate = lax.dot_general(qr.astype(BF16), km.astype(BF16), (((1,), (1,)), ((), ())),
                               preferred_element_type=F32)
        g = jnp.where(ids < n_past, gate, -jnp.inf)
        sel = jnp.zeros(gate.shape, jnp.bool_)
        for r in range(MOBA_TOPK):
            mx = jnp.max(g, axis=-1, keepdims=True)
            first = jnp.min(jnp.where(g == mx, ids, float(LANES)), axis=-1, keepdims=True)
            pick = ids == first
            sel = jnp.logical_or(sel, jnp.logical_and(pick, n_past > float(r)))
            g = jnp.where(pick, -jnp.inf, g)

        qa_ref[:, lo] = (qr * scale).astype(BF16)
        qa_ref[:, hi] = jnp.where(sel, 0.0, MASK_BIAS).astype(BF16)
        ka_ref[:, lo] = kr.astype(BF16)
        ka_ref[:, hi] = onehot_n
        km_ref[:, sl] = jnp.where(km_row == n, jnp.mean(kr, axis=0, keepdims=True), km)
    vt_ref[...] = v_ref[...].T.astype(BF16)


def _prep(proj, rope_c, rope_a, rope_b, batch, seq, qcol):
    t = proj.shape[0]
    w = N_HEADS * HEAD_DIM
    nb = seq // MOBA_BLOCK
    assert nb <= LANES
    blk = lambda c: pl.BlockSpec((MOBA_BLOCK, w), lambda b, n: (b * nb + n, c))
    tab = pl.BlockSpec((MOBA_BLOCK, HEAD_DIM), lambda b, n: (n, 0))
    out = pl.BlockSpec((MOBA_BLOCK, 2 * w), lambda b, n: (b * nb + n, 0))
    return pl.pallas_call(
        _prep_kernel,
        grid=(batch, nb),
        in_specs=[blk(qcol), blk(qcol + 1), blk(qcol + 2), tab, tab, tab],
        out_specs=[out, out, pl.BlockSpec((w, MOBA_BLOCK), lambda b, n: (b, n))],
        out_shape=[jax.ShapeDtypeStruct((t, 2 * w), BF16), jax.ShapeDtypeStruct((t, 2 * w), BF16),
                   jax.ShapeDtypeStruct((batch * w, seq), BF16)],
        scratch_shapes=[pltpu.VMEM((LANES, w), F32)],
        compiler_params=_params("arbitrary", "arbitrary"),
        name="rope_prep",
    )(proj, proj, proj, rope_c, rope_a, rope_b)


def _attn_kernel(qa_ref, ka_ref, vt_ref, o_ref, s_ref, *, group, heads):
    qb = pl.program_id(2)
    blk = o_ref.shape[0]
    aug = HEAD_DIM + LANES
    nt = (((1,), (1,)), ((), ()))
    start = pl.multiple_of(qb * blk, blk)
    key = lax.broadcasted_iota(jnp.int32, (blk, blk), 0)
    qry = lax.broadcasted_iota(jnp.int32, (blk, blk), 1)

    def init(h):
        q = qa_ref[:, h * aug:h * aug + HEAD_DIM]
        k_own = ka_ref[pl.ds(start, blk), h * aug:h * aug + HEAD_DIM]
        s = lax.dot_general(k_own, q, nt, preferred_element_type=F32)
        s = jnp.where(key <= qry, s, -jnp.inf)
        m = jnp.max(s, axis=0, keepdims=True)
        p = jnp.exp2(s - m)
        l = jnp.sum(p, axis=0, keepdims=True)
        v_own = vt_ref[h * HEAD_DIM:(h + 1) * HEAD_DIM, pl.ds(start, blk)]
        return m, l, jnp.dot(v_own, p.astype(BF16), preferred_element_type=F32)

    span = group * blk
    last_off = ka_ref.shape[0] - span

    def logits_to(buf, h, g):
        off = pl.multiple_of(jnp.minimum(g * span, last_off), span)
        k_grp = ka_ref[pl.ds(off, span), h * aug:(h + 1) * aug]
        s_ref[buf, h] = lax.dot_general(k_grp, qa_ref[:, h * aug:(h + 1) * aug], nt, preferred_element_type=F32)

    def update(buf, h, g, state):
        m, l, acc = state
        off = pl.multiple_of(g * span, span)
        m_new = jnp.maximum(m, jnp.max(s_ref[buf, h], axis=0, keepdims=True))
        p = jnp.exp2(s_ref[buf, h] - m_new)
        corr = jnp.exp2(m - m_new)
        l_new = corr * l + jnp.sum(p, axis=0, keepdims=True)
        v_grp = vt_ref[h * HEAD_DIM:(h + 1) * HEAD_DIM, pl.ds(off, span)]
        return m_new, l_new, corr * acc + jnp.dot(v_grp, p.astype(BF16), preferred_element_type=F32)

    def body(t, states):
        for h in range(heads):
            logits_to(1, h, 2 * t + 1)
        states = tuple(update(0, h, 2 * t, states[h]) for h in range(heads))
        for h in range(heads):
            logits_to(0, h, 2 * t + 2)
        return tuple(update(1, h, 2 * t + 1, states[h]) for h in range(heads))

    n_trips = (qb + 2 * group - 1) // (2 * group)
    for h in range(heads):
        logits_to(0, h, 0)
    states = lax.fori_loop(0, n_trips, body, tuple(init(h) for h in range(heads)))
    for h in range(heads):
        m, l, acc = states[h]
        o_ref[:, h * HEAD_DIM:(h + 1) * HEAD_DIM] = (acc / l).T.astype(o_ref.dtype)


def _attention(qa, ka, vt, batch, seq, group=4, heads=2):
    t = qa.shape[0]
    w = N_HEADS * HEAD_DIM
    nb = seq // MOBA_BLOCK
    aug = HEAD_DIM + LANES
    assert nb % (2 * group) == 0 and N_HEADS % heads == 0
    hg = N_HEADS // heads
    kern = functools.partial(_attn_kernel, group=group, heads=heads)
    return pl.pallas_call(
        kern,
        grid=(batch, hg, nb),
        in_specs=[pl.BlockSpec((MOBA_BLOCK, heads * aug), lambda b, h, n: (b * nb + n, h)),
                  pl.BlockSpec((seq, heads * aug), lambda b, h, n: (b, h), pipeline_mode=pl.Buffered(1)),
                  pl.BlockSpec((heads * HEAD_DIM, seq), lambda b, h, n: (b * hg + h, 0),
                               pipeline_mode=pl.Buffered(1))],
        out_specs=pl.BlockSpec((MOBA_BLOCK, heads * HEAD_DIM), lambda b, h, n: (b * nb + n, h)),
        out_shape=jax.ShapeDtypeStruct((t, w), BF16),
        scratch_shapes=[pltpu.VMEM((2, heads, group * MOBA_BLOCK, MOBA_BLOCK), F32)],
        compiler_params=_params("arbitrary", "arbitrary", "arbitrary"),
        name="moba_attn",
    )(qa, ka, vt)


def _merge_kernel(hc_ref, o_ref, gc0_ref, gc1_ref, ga0_ref, ga1_ref, x_ref, g1_ref, sc2_ref, sh2_ref,
                  wc_ref, bc_ref, wa_ref, wo_ref, lg_ref, lb_ref, x1_ref, u2_ref):
    yc = jnp.dot(hc_ref[...], wc_ref[...], preferred_element_type=F32) + bc_ref[...]
    ya = jnp.dot(o_ref[...], wa_ref[...], preferred_element_type=F32)
    gc = jax.nn.sigmoid(jnp.concatenate([gc0_ref[...], gc1_ref[...]], axis=1))
    ga = jax.nn.sigmoid(jnp.concatenate([ga0_ref[...], ga1_ref[...]], axis=1))
    mix = (gc * yc + ga * ya).astype(BF16)
    y = jnp.dot(mix, wo_ref[...], preferred_element_type=F32)
    z = DEEPNORM_ALPHA * x_ref[...] + (1.0 + g1_ref[0]) * y
    x1 = _layer_norm(z, lg_ref[...], lb_ref[...])
    x1_ref[...] = x1
    u2_ref[...] = (x1 * (1.0 + sc2_ref[0]) + sh2_ref[0]).astype(BF16)


def _merge(hc, o, proj, x2d, gate1, scale2, shift2, wc, bc, wa, wo, lg, lb, seq, gcol, tm=256):
    t, d = x2d.shape
    ch = hc.shape[1]
    w = o.shape[1]
    tpb = seq // tm
    row = lambda cols, c: pl.BlockSpec((tm, cols), lambda i: (i, c))
    mod = pl.BlockSpec((1, 1, d), lambda i: (i // tpb, 0, 0))
    return pl.pallas_call(
        _merge_kernel,
        grid=(t // tm,),
        in_specs=[row(ch, 0), row(w, 0),
                  row(d // 2, gcol), row(d // 2, gcol + 1), row(d // 2, gcol + 2), row(d // 2, gcol + 3),
                  row(d, 0), mod, mod, mod,
                  _const_spec(wc.shape), _const_spec((1, d)), _const_spec(wa.shape), _const_spec(wo.shape),
                  _const_spec((1, d)), _const_spec((1, d))],
        out_specs=[row(d, 0), row(d, 0)],
        out_shape=[jax.ShapeDtypeStruct((t, d), F32), jax.ShapeDtypeStruct((t, d), BF16)],
        compiler_params=_params("arbitrary"),
        name="merge_ln1",
    )(hc, o, proj, proj, proj, proj, x2d, gate1, scale2, shift2, wc, bc, wa, wo, lg, lb)


def _ffn_up_kernel(u_ref, wa_ref, wv_ref, cw_ref, cb_ref, h_ref, abuf_ref, carry_ref, *, tm, tpb, rc):
    i = pl.program_id(0)
    f = pl.program_id(1)

    first_of_batch = (i % tpb) == 0

    @pl.when(first_of_batch)
    def _():
        abuf_ref[0:FFN_HALO, :] = jnp.zeros((FFN_HALO, abuf_ref.shape[1]), F32)

    @pl.when(jnp.logical_not(first_of_batch))
    def _():
        abuf_ref[0:FFN_HALO, :] = carry_ref[f]

    for r0 in range(0, tm, rc):
        u = u_ref[r0:r0 + rc, :]
        a = jnp.dot(u, wa_ref[...], preferred_element_type=F32)
        val = jnp.dot(u, wv_ref[...], preferred_element_type=F32)
        abuf_ref[FFN_HALO + r0:FFN_HALO + r0 + rc, :] = a
        conv = a * cw_ref[FFN_CONV - 1:FFN_CONV, :] + cb_ref[...]
        for j in range(FFN_CONV - 1):
            start = FFN_HALO + r0 - (FFN_CONV - 1 - j)
            conv = conv + abuf_ref[start:start + rc, :] * cw_ref[j:j + 1, :]
        h_ref[r0:r0 + rc, :] = (_silu(conv) * val).astype(h_ref.dtype)
    carry_ref[f] = abuf_ref[tm:tm + FFN_HALO, :]


def _ffn_up(u2, w_up, cw, cb, dff, seq, tm, tf, rc=256):
    t, d = u2.shape
    nf = dff // tf
    tpb = seq // tm
    kern = functools.partial(_ffn_up_kernel, tm=tm, tpb=tpb, rc=min(rc, tm))
    return pl.pallas_call(
        kern,
        grid=(t // tm, nf),
        in_specs=[pl.BlockSpec((tm, d), lambda i, f: (i, 0)),
                  pl.BlockSpec((d, tf), lambda i, f: (0, f)),
                  pl.BlockSpec((d, tf), lambda i, f: (0, nf + f)),
                  pl.BlockSpec((FFN_CONV, tf), lambda i, f: (0, f)),
                  pl.BlockSpec((1, tf), lambda i, f: (0, f))],
        out_specs=pl.BlockSpec((tm, tf), lambda i, f: (i, f)),
        out_shape=jax.ShapeDtypeStruct((t, dff), BF16),
        scratch_shapes=[pltpu.VMEM((tm + FFN_HALO, tf), F32),
                        pltpu.VMEM((nf, FFN_HALO, tf), F32)],
        compiler_params=_params("arbitrary", "arbitrary"),
        name="convffn_up",
    )(u2, w_up, w_up, cw, cb)


def _ffn_down_kernel(h_ref, wd_ref, x1_ref, g2_ref, lg_ref, lb_ref, o_ref, *, rc):
    for r0 in range(0, h_ref.shape[0], rc):
        y = jnp.dot(h_ref[r0:r0 + rc, :], wd_ref[...], preferred_element_type=F32)
        z = DEEPNORM_ALPHA * x1_ref[r0:r0 + rc, :] + (1.0 + g2_ref[0]) * y
        o_ref[r0:r0 + rc, :] = _layer_norm(z, lg_ref[...], lb_ref[...])


def _ffn_down(hid, w_down, x1, gate2, lg, lb, seq, tm, rc=256):
    t, dff = hid.shape
    d = w_down.shape[1]
    tpb = seq // tm
    return pl.pallas_call(
        functools.partial(_ffn_down_kernel, rc=min(rc, tm)),
        grid=(t // tm,),
        in_specs=[pl.BlockSpec((tm, dff), lambda i: (i, 0)),
                  _const_spec(w_down.shape),
                  pl.BlockSpec((tm, d), lambda i: (i, 0)),
                  pl.BlockSpec((1, 1, d), lambda i: (i // tpb, 0, 0)),
                  _const_spec((1, d)), _const_spec((1, d))],
        out_specs=pl.BlockSpec((tm, d), lambda i: (i, 0)),
        out_shape=jax.ShapeDtypeStruct((t, d), F32),
        compiler_params=_params("arbitrary"),
        name="ffn_down_ln2",
    )(hid, w_down, x1, gate2, lg, lb)


def _rope_tables(seq):
    half = ROT_DIM // 2
    pos = jnp.arange(seq, dtype=F32)
    inv_freq = ROPE_THETA ** (-jnp.arange(0, ROT_DIM, 2, dtype=F32) / ROT_DIM)
    ang = pos[:, None] * inv_freq[None, :]
    cos, sin = jnp.cos(ang), jnp.sin(ang)
    ones = jnp.ones((seq, HEAD_DIM - ROT_DIM), F32)
    zeros = jnp.zeros((seq, HEAD_DIM - ROT_DIM), F32)
    zh = jnp.zeros((seq, half), F32)
    tab_c = jnp.concatenate([cos, cos, ones], axis=1)
    tab_a = jnp.concatenate([-sin, zh, zeros], axis=1)
    tab_b = jnp.concatenate([zh, sin, zeros], axis=1)
    return tab_c, tab_a, tab_b


def kernel(x, c, w_ada, b_ada, w_in, conv_dw_w, conv_dw_b, conv_ln_g, conv_ln_b, w_conv_out, b_conv_out,
           w_attn_out, w_out, ln1_g, ln1_b, w_up, ffn_dw_w, ffn_dw_b, w_down, ln2_g, ln2_b):
    batch, seq, d = x.shape
    ch = conv_dw_w.shape[-1]
    width = N_HEADS * HEAD_DIM
    assert seq % MOBA_BLOCK == 0 and w_ada.shape[0] == DEPTH
    assert 2 * ch % width == 0 and (2 * ch + 3 * width) % (d // 2) == 0
    x2d = x.reshape(batch * seq, d)
    row = lambda v: v.reshape(1, -1)

    c_pad = jnp.pad(c, ((0, SUBLANES - batch), (0, 0)))
    mod = _ada(c_pad, w_ada[0], row(b_ada[0]))[:batch]
    shift1, scale1, gate1, shift2, scale2, gate2 = [m.reshape(batch, 1, d) for m in jnp.split(mod, 6, axis=-1)]

    proj = _inproj(x2d, scale1, shift1, w_in[0].astype(BF16), seq, tm=min(1024, seq), tn=1024)

    hc = _conv_branch(proj, conv_dw_w[0], row(conv_dw_b[0]), row(conv_ln_g[0]), row(conv_ln_b[0]),
                      batch, seq, ch, tm=min(256, seq))

    tab_c, tab_a, tab_b = _rope_tables(seq)
    qa, ka, vt = _prep(proj, tab_c, tab_a, tab_b, batch, seq, qcol=2 * ch // width)
    o = _attention(qa, ka, vt, batch, seq, group=min(4, seq // MOBA_BLOCK))

    x1, u2 = _merge(hc, o, proj, x2d, gate1, scale2, shift2,
                    w_conv_out[0].astype(BF16), row(b_conv_out[0]), w_attn_out[0].astype(BF16),
                    w_out[0].astype(BF16), row(ln1_g[0]), row(ln1_b[0]), seq,
                    gcol=(2 * ch + 3 * width) // (d // 2))

    hid = _ffn_up(u2, w_up[0].astype(BF16), ffn_dw_w[0], row(ffn_dw_b[0]), w_down.shape[1], seq,
                  tm=min(1024, seq), tf=512)
    out = _ffn_down(hid, w_down[0].astype(BF16), x1, gate2, row(ln2_g[0]), row(ln2_b[0]), seq, tm=min(512, seq))
    return out.reshape(batch, seq, d)
```

```python
import functools

import jax
import jax.numpy as jnp
from jax import lax
from jax.experimental import pallas as pl
from jax.experimental.pallas import tpu as pltpu

F32 = jnp.float32
BF16 = jnp.bfloat16

N_HEADS = 8
HEAD_DIM = 128
ROT_DIM = HEAD_DIM // 4
ROPE_THETA = 500000.0
MOBA_BLOCK = 256
MOBA_TOPK = 3
CONV_KERNEL = 31
FFN_CONV = 3
LN_EPS = 1e-5
DEPTH = 1
DEEPNORM_ALPHA = (2.0 * DEPTH) ** 0.25

LANES = 128
SUBLANES = 8
MASK_BIAS = -(2.0 ** 30)
LOG2_E = 1.4426950408889634
CONV_HALO = 32
FFN_HALO = SUBLANES
VMEM_LIMIT = 56 * 1024 * 1024


def _params(*sem):
    return pltpu.CompilerParams(dimension_semantics=sem, vmem_limit_bytes=VMEM_LIMIT)


def _const_spec(shape):
    nd = len(shape)
    return pl.BlockSpec(shape, lambda *_: (0,) * nd, pipeline_mode=pl.Buffered(1))


def _layer_norm(z, g, b):
    mu = jnp.mean(z, axis=-1, keepdims=True)
    zc = z - mu
    var = jnp.mean(zc * zc, axis=-1, keepdims=True)
    return zc * lax.rsqrt(var + LN_EPS) * g + b


def _silu(v):
    return v * jax.nn.sigmoid(v)


def _ada_kernel(c_ref, w_ref, b_ref, o_ref):
    ca = _silu(c_ref[...]).astype(BF16)
    o_ref[...] = jnp.dot(ca, w_ref[...].astype(BF16), preferred_element_type=F32) + b_ref[...]


def _ada(c_pad, w_ada, b_ada, tn=1024):
    rows, d = c_pad.shape
    n = w_ada.shape[1]
    return pl.pallas_call(
        _ada_kernel,
        grid=(n // tn,),
        in_specs=[pl.BlockSpec((rows, d), lambda j: (0, 0)),
                  pl.BlockSpec((d, tn), lambda j: (0, j)),
                  pl.BlockSpec((1, tn), lambda j: (0, j))],
        out_specs=pl.BlockSpec((rows, tn), lambda j: (0, j)),
        out_shape=jax.ShapeDtypeStruct((rows, n), F32),
        compiler_params=_params("arbitrary"),
        name="ada",
    )(c_pad, w_ada, b_ada)


def _inproj_kernel(x_ref, sc_ref, sh_ref, w_ref, o_ref, u_ref):
    @pl.when(pl.program_id(1) == 0)
    def _():
        u_ref[...] = (x_ref[...] * (1.0 + sc_ref[0]) + sh_ref[0]).astype(BF16)

    o_ref[...] = jnp.dot(u_ref[...], w_ref[...], preferred_element_type=F32)


def _inproj(x2d, scale, shift, w, seq, tm, tn):
    t, d = x2d.shape
    n = w.shape[1]
    tpb = seq // tm
    return pl.pallas_call(
        _inproj_kernel,
        grid=(t // tm, n // tn),
        in_specs=[pl.BlockSpec((tm, d), lambda i, j: (i, 0)),
                  pl.BlockSpec((1, 1, d), lambda i, j: (i // tpb, 0, 0)),
                  pl.BlockSpec((1, 1, d), lambda i, j: (i // tpb, 0, 0)),
                  pl.BlockSpec((d, tn), lambda i, j: (0, j))],
        out_specs=pl.BlockSpec((tm, tn), lambda i, j: (i, j)),
        out_shape=jax.ShapeDtypeStruct((t, n), F32),
        scratch_shapes=[pltpu.VMEM((tm, d), BF16)],
        compiler_params=_params("arbitrary", "arbitrary"),
        name="inproj",
    )(x2d, scale, shift, w)


def _conv_kernel(al_ref, ag_ref, w_ref, b_ref, g_ref, be_ref, o_ref, win_ref, wb_ref, y_ref, *,
                 tm, rows, lanes, ln_rows):
    i = pl.program_id(1)

    ch = win_ref.shape[2]

    @pl.when(jnp.logical_and(pl.program_id(0) == 0, i == 0))
    def _():
        for j in range(CONV_KERNEL):
            wb_ref[j] = jnp.broadcast_to(w_ref[j:j + 1, :], (SUBLANES, ch))
        wb_ref[CONV_KERNEL] = jnp.broadcast_to(b_ref[...], (SUBLANES, ch))

    @pl.when(i == 0)
    def _():
        win_ref[0, 0:CONV_HALO, :] = jnp.zeros((CONV_HALO, ch), F32)

    @pl.when(i > 0)
    def _():
        win_ref[0, 0:CONV_HALO, :] = win_ref[0, tm:tm + CONV_HALO, :]

    win_ref[0, CONV_HALO:CONV_HALO + tm, :] = al_ref[...] * jax.nn.sigmoid(ag_ref[...])

    n_shift = tm + CONV_HALO - SUBLANES
    for s in range(1, SUBLANES):
        win_ref[s, 0:n_shift, :] = win_ref[0, s:s + n_shift, :]

    first = CONV_HALO - (CONV_KERNEL - 1)

    def body(s, carry):
        r0 = pl.multiple_of(s * rows, rows)
        for c0 in range(0, ch, lanes):
            cs = slice(c0, c0 + lanes)
            accs = [wb_ref[CONV_KERNEL, :, cs]] * (rows // SUBLANES)
            for j in range(CONV_KERNEL):
                tiles, rem = divmod(first + j, SUBLANES)
                wj = wb_ref[j, :, cs]
                for k in range(rows // SUBLANES):
                    start = pl.multiple_of(r0 + (tiles + k) * SUBLANES, SUBLANES)
                    accs[k] = accs[k] + win_ref[rem, pl.ds(start, SUBLANES), cs] * wj
            y_ref[pl.ds(r0, rows), cs] = jnp.concatenate(accs, axis=0)
        for rr in range(0, rows, ln_rows):
            y = y_ref[pl.ds(pl.multiple_of(r0 + rr, ln_rows), ln_rows), :]
            y = _silu(_layer_norm(y, g_ref[...], be_ref[...]))
            o_ref[pl.ds(pl.multiple_of(r0 + rr, ln_rows), ln_rows), :] = y.astype(o_ref.dtype)
        return carry

    lax.fori_loop(0, tm // rows, body, 0)


def _conv_branch(proj, dw_w, dw_b, ln_g, ln_b, batch, seq, ch, tm=256, rows=128, lanes=256, ln_rows=32):
    t = proj.shape[0]
    tpb = seq // tm
    kern = functools.partial(_conv_kernel, tm=tm, rows=rows, lanes=lanes, ln_rows=ln_rows)
    return pl.pallas_call(
        kern,
        grid=(batch, tpb),
        in_specs=[pl.BlockSpec((tm, ch), lambda b, i: (b * tpb + i, 0)),
                  pl.BlockSpec((tm, ch), lambda b, i: (b * tpb + i, 1)),
                  _const_spec(dw_w.shape), _const_spec((1, ch)), _const_spec((1, ch)), _const_spec((1, ch))],
        out_specs=pl.BlockSpec((tm, ch), lambda b, i: (b * tpb + i, 0)),
        out_shape=jax.ShapeDtypeStruct((t, ch), BF16),
        scratch_shapes=[pltpu.VMEM((SUBLANES, tm + CONV_HALO, ch), F32),
                        pltpu.VMEM((CONV_KERNEL + 1, SUBLANES, ch), F32),
                        pltpu.VMEM((tm, ch), F32)],
        compiler_params=_params("arbitrary", "arbitrary"),
        name="conv_branch",
    )(proj, proj, dw_w, dw_b, ln_g, ln_b)


def _prep_kernel(q_ref, k_ref, v_ref, ct_ref, st_ref, c_ref, a_ref, b_ref, qa_ref, ka_ref, vt_ref, km_ref):
    n = pl.program_id(1)
    blk = q_ref.shape[0]
    half = ROT_DIM // 2

    @pl.when(n == 0)
    def _():
        km_ref[...] = jnp.zeros(km_ref.shape, F32)

    cc, aa, bb = c_ref[...], a_ref[...], b_ref[...]
    ct, st = ct_ref[...], st_ref[...]

    def rope(xh):
        return xh * cc + pltpu.roll(xh, LANES - half, 1) * aa + pltpu.roll(xh, half, 1) * bb

    def rope_t(xt):
        partner = jnp.concatenate([xt[half:ROT_DIM], xt[:half], xt[ROT_DIM:]], axis=0)
        return xt * ct + partner * st

    lane = lax.broadcasted_iota(jnp.int32, (blk, LANES), 1)
    onehot_n = jnp.where(lane == n, 1.0, 0.0).astype(BF16)
    ids = lax.broadcasted_iota(jnp.int32, (LANES, blk), 0).astype(F32)
    n_past = jnp.full((LANES, blk), n.astype(F32))
    km_row = lax.broadcasted_iota(jnp.int32, (km_ref.shape[0], HEAD_DIM), 0)
    scale = HEAD_DIM ** -0.5 * LOG2_E
    aug = HEAD_DIM + LANES

    for h in range(N_HEADS):
        sl = slice(h * HEAD_DIM, (h + 1) * HEAD_DIM)
        qt = rope_t(q_ref[:, sl].T)
        kr = rope(k_ref[:, sl])
        km = km_ref[:, sl]

        gate = jnp.dot(km.astype(BF16), qt.astype(BF16), preferred_element_type=F32)
        g = jnp.where(ids < n_past, gate, -jnp.inf)
        sel = jnp.zeros(gate.shape, jnp.bool_)
        for r in range(MOBA_TOPK):
            mx = jnp.max(g, axis=0, keepdims=True)
            first = jnp.min(jnp.where(g == mx, ids, float(LANES)), axis=0, keepdims=True)
            pick = ids == first
            sel = jnp.logical_or(sel, jnp.logical_and(pick, n_past > float(r)))
            g = jnp.where(pick, -jnp.inf, g)

        qa_ref[h * aug:h * aug + HEAD_DIM, :] = (qt * scale).astype(BF16)
        qa_ref[h * aug + HEAD_DIM:(h + 1) * aug, :] = jnp.where(sel, 0.0, MASK_BIAS).astype(BF16)
        ka_ref[:, h * aug:h * aug + HEAD_DIM] = kr.astype(BF16)
        ka_ref[:, h * aug + HEAD_DIM:(h + 1) * aug] = onehot_n
        km_ref[:, sl] = jnp.where(km_row == n, jnp.mean(kr, axis=0, keepdims=True), km)
    vt_ref[...] = v_ref[...].T.astype(BF16)


def _prep(proj, tabs_t, tabs, batch, seq, qcol):
    t = proj.shape[0]
    w = N_HEADS * HEAD_DIM
    nb = seq // MOBA_BLOCK
    assert nb <= LANES
    blk = lambda c: pl.BlockSpec((MOBA_BLOCK, w), lambda b, n: (b * nb + n, c))
    tab = pl.BlockSpec((MOBA_BLOCK, HEAD_DIM), lambda b, n: (n, 0))
    tab_t = pl.BlockSpec((HEAD_DIM, MOBA_BLOCK), lambda b, n: (0, n))
    return pl.pallas_call(
        _prep_kernel,
        grid=(batch, nb),
        in_specs=[blk(qcol), blk(qcol + 1), blk(qcol + 2), tab_t, tab_t, tab, tab, tab],
        out_specs=[pl.BlockSpec((2 * w, MOBA_BLOCK), lambda b, n: (b, n)),
                   pl.BlockSpec((MOBA_BLOCK, 2 * w), lambda b, n: (b * nb + n, 0)),
                   pl.BlockSpec((w, MOBA_BLOCK), lambda b, n: (b, n))],
        out_shape=[jax.ShapeDtypeStruct((batch * 2 * w, seq), BF16), jax.ShapeDtypeStruct((t, 2 * w), BF16),
                   jax.ShapeDtypeStruct((batch * w, seq), BF16)],
        scratch_shapes=[pltpu.VMEM((LANES, w), F32)],
        compiler_params=_params("arbitrary", "arbitrary"),
        name="rope_prep",
    )(proj, proj, proj, *tabs_t, *tabs)


def _attn_kernel(qa_ref, ka_ref, vt_ref, o_ref, s_ref, *, group, heads):
    qb = pl.program_id(2)
    blk = o_ref.shape[0]
    aug = HEAD_DIM + LANES
    start = pl.multiple_of(qb * blk, blk)
    key = lax.broadcasted_iota(jnp.int32, (blk, blk), 0)
    qry = lax.broadcasted_iota(jnp.int32, (blk, blk), 1)

    def init(h):
        q_t = qa_ref[h * aug:h * aug + HEAD_DIM, :]
        k_own = ka_ref[pl.ds(start, blk), h * aug:h * aug + HEAD_DIM]
        s = jnp.dot(k_own, q_t, preferred_element_type=F32)
        s = jnp.where(key <= qry, s, -jnp.inf)
        m = jnp.max(s, axis=0, keepdims=True)
        p = jnp.exp2(s - m)
        l = jnp.sum(p, axis=0, keepdims=True)
        v_own = vt_ref[h * HEAD_DIM:(h + 1) * HEAD_DIM, pl.ds(start, blk)]
        return m, l, jnp.dot(v_own, p.astype(BF16), preferred_element_type=F32)

    span = group * blk
    last_off = ka_ref.shape[0] - span

    def logits_to(buf, h, g):
        off = pl.multiple_of(jnp.minimum(g * span, last_off), span)
        k_grp = ka_ref[pl.ds(off, span), h * aug:(h + 1) * aug]
        s_ref[buf, h] = jnp.dot(k_grp, qa_ref[h * aug:(h + 1) * aug, :], preferred_element_type=F32)

    def update(buf, h, g, state):
        m, l, acc = state
        off = pl.multiple_of(g * span, span)
        m_new = jnp.maximum(m, jnp.max(s_ref[buf, h], axis=0, keepdims=True))
        p = jnp.exp2(s_ref[buf, h] - m_new)
        corr = jnp.exp2(m - m_new)
        l_new = corr * l + jnp.sum(p, axis=0, keepdims=True)
        v_grp = vt_ref[h * HEAD_DIM:(h + 1) * HEAD_DIM, pl.ds(off, span)]
        return m_new, l_new, corr * acc + jnp.dot(v_grp, p.astype(BF16), preferred_element_type=F32)

    def body(t, states):
        for h in range(heads):
            logits_to(1, h, 2 * t + 1)
        states = tuple(update(0, h, 2 * t, states[h]) for h in range(heads))
        for h in range(heads):
            logits_to(0, h, 2 * t + 2)
        return tuple(update(1, h, 2 * t + 1, states[h]) for h in range(heads))

    n_trips = (qb + 2 * group - 1) // (2 * group)
    for h in range(heads):
        logits_to(0, h, 0)
    states = lax.fori_loop(0, n_trips, body, tuple(init(h) for h in range(heads)))
    for h in range(heads):
        m, l, acc = states[h]
        o_ref[:, h * HEAD_DIM:(h + 1) * HEAD_DIM] = (acc / l).T.astype(o_ref.dtype)


def _attention(qa, ka, vt, batch, seq, group=4, heads=2):
    t = ka.shape[0]
    w = N_HEADS * HEAD_DIM
    nb = seq // MOBA_BLOCK
    aug = HEAD_DIM + LANES
    assert nb % (2 * group) == 0 and N_HEADS % heads == 0
    hg = N_HEADS // heads
    kern = functools.partial(_attn_kernel, group=group, heads=heads)
    return pl.pallas_call(
        kern,
        grid=(batch, hg, nb),
        in_specs=[pl.BlockSpec((heads * aug, MOBA_BLOCK), lambda b, h, n: (b * hg + h, n)),
                  pl.BlockSpec((seq, heads * aug), lambda b, h, n: (b, h), pipeline_mode=pl.Buffered(1)),
                  pl.BlockSpec((heads * HEAD_DIM, seq), lambda b, h, n: (b * hg + h, 0),
                               pipeline_mode=pl.Buffered(1))],
        out_specs=pl.BlockSpec((MOBA_BLOCK, heads * HEAD_DIM), lambda b, h, n: (b * nb + n, h)),
        out_shape=jax.ShapeDtypeStruct((t, w), BF16),
        scratch_shapes=[pltpu.VMEM((2, heads, group * MOBA_BLOCK, MOBA_BLOCK), F32)],
        compiler_params=_params("arbitrary", "arbitrary", "arbitrary"),
        name="moba_attn",
    )(qa, ka, vt)


def _merge_kernel(hc_ref, o_ref, gc0_ref, gc1_ref, ga0_ref, ga1_ref, x_ref, g1_ref, sc2_ref, sh2_ref,
                  wc_ref, bc_ref, wa_ref, wo_ref, lg_ref, lb_ref, x1_ref, u2_ref):
    yc = jnp.dot(hc_ref[...], wc_ref[...], preferred_element_type=F32) + bc_ref[...]
    ya = jnp.dot(o_ref[...], wa_ref[...], preferred_element_type=F32)
    gc = jax.nn.sigmoid(jnp.concatenate([gc0_ref[...], gc1_ref[...]], axis=1))
    ga = jax.nn.sigmoid(jnp.concatenate([ga0_ref[...], ga1_ref[...]], axis=1))
    mix = (gc * yc + ga * ya).astype(BF16)
    y = jnp.dot(mix, wo_ref[...], preferred_element_type=F32)
    z = DEEPNORM_ALPHA * x_ref[...] + (1.0 + g1_ref[0]) * y
    x1 = _layer_norm(z, lg_ref[...], lb_ref[...])
    x1_ref[...] = x1
    u2_ref[...] = (x1 * (1.0 + sc2_ref[0]) + sh2_ref[0]).astype(BF16)


def _merge(hc, o, proj, x2d, gate1, scale2, shift2, wc, bc, wa, wo, lg, lb, seq, gcol, tm=256):
    t, d = x2d.shape
    ch = hc.shape[1]
    w = o.shape[1]
    tpb = seq // tm
    row = lambda cols, c: pl.BlockSpec((tm, cols), lambda i: (i, c))
    mod = pl.BlockSpec((1, 1, d), lambda i: (i // tpb, 0, 0))
    return pl.pallas_call(
        _merge_kernel,
        grid=(t // tm,),
        in_specs=[row(ch, 0), row(w, 0),
                  row(d // 2, gcol), row(d // 2, gcol + 1), row(d // 2, gcol + 2), row(d // 2, gcol + 3),
                  row(d, 0), mod, mod, mod,
                  _const_spec(wc.shape), _const_spec((1, d)), _const_spec(wa.shape), _const_spec(wo.shape),
                  _const_spec((1, d)), _const_spec((1, d))],
        out_specs=[row(d, 0), row(d, 0)],
        out_shape=[jax.ShapeDtypeStruct((t, d), F32), jax.ShapeDtypeStruct((t, d), BF16)],
        compiler_params=_params("arbitrary"),
        name="merge_ln1",
    )(hc, o, proj, proj, proj, proj, x2d, gate1, scale2, shift2, wc, bc, wa, wo, lg, lb)


def _ffn_up_kernel(u_ref, wa_ref, wv_ref, cw_ref, cb_ref, h_ref, abuf_ref, carry_ref, *, tm, tpb, rc):
    i = pl.program_id(0)
    f = pl.program_id(1)

    first_of_batch = (i % tpb) == 0

    @pl.when(first_of_batch)
    def _():
        abuf_ref[0:FFN_HALO, :] = jnp.zeros((FFN_HALO, abuf_ref.shape[1]), F32)

    @pl.when(jnp.logical_not(first_of_batch))
    def _():
        abuf_ref[0:FFN_HALO, :] = carry_ref[f]

    for r0 in range(0, tm, rc):
        u = u_ref[r0:r0 + rc, :]
        a = jnp.dot(u, wa_ref[...], preferred_element_type=F32)
        val = jnp.dot(u, wv_ref[...], preferred_element_type=F32)
        abuf_ref[FFN_HALO + r0:FFN_HALO + r0 + rc, :] = a
        conv = a * cw_ref[FFN_CONV - 1:FFN_CONV, :] + cb_ref[...]
        for j in range(FFN_CONV - 1):
            start = FFN_HALO + r0 - (FFN_CONV - 1 - j)
            conv = conv + abuf_ref[start:start + rc, :] * cw_ref[j:j + 1, :]
        h_ref[r0:r0 + rc, :] = (_silu(conv) * val).astype(h_ref.dtype)
    carry_ref[f] = abuf_ref[tm:tm + FFN_HALO, :]


def _ffn_up(u2, w_up, cw, cb, dff, seq, tm, tf, rc=256):
    t, d = u2.shape
    nf = dff // tf
    tpb = seq // tm
    kern = functools.partial(_ffn_up_kernel, tm=tm, tpb=tpb, rc=min(rc, tm))
    return pl.pallas_call(
        kern,
        grid=(t // tm, nf),
        in_specs=[pl.BlockSpec((tm, d), lambda i, f: (i, 0)),
                  pl.BlockSpec((d, tf), lambda i, f: (0, f)),
                  pl.BlockSpec((d, tf), lambda i, f: (0, nf + f)),
                  pl.BlockSpec((FFN_CONV, tf), lambda i, f: (0, f)),
                  pl.BlockSpec((1, tf), lambda i, f: (0, f))],
        out_specs=pl.BlockSpec((tm, tf), lambda i, f: (i, f)),
        out_shape=jax.ShapeDtypeStruct((t, dff), BF16),
        scratch_shapes=[pltpu.VMEM((tm + FFN_HALO, tf), F32),
                        pltpu.VMEM((nf, FFN_HALO, tf), F32)],
        compiler_params=_params("arbitrary", "arbitrary"),
        name="convffn_up",
    )(u2, w_up, w_up, cw, cb)


def _ffn_down_kernel(h_ref, wd_ref, x1_ref, g2_ref, lg_ref, lb_ref, o_ref, *, rc):
    for r0 in range(0, h_ref.shape[0], rc):
        y = jnp.dot(h_ref[r0:r0 + rc, :], wd_ref[...], preferred_element_type=F32)
        z = DEEPNORM_ALPHA * x1_ref[r0:r0 + rc, :] + (1.0 + g2_ref[0]) * y
        o_ref[r0:r0 + rc, :] = _layer_norm(z, lg_ref[...], lb_ref[...])


def _ffn_down(hid, w_down, x1, gate2, lg, lb, seq, tm, rc=256):
    t, dff = hid.shape
    d = w_down.shape[1]
    tpb = seq // tm
    return pl.pallas_call(
        functools.partial(_ffn_down_kernel, rc=min(rc, tm)),
        grid=(t // tm,),
        in_specs=[pl.BlockSpec((tm, dff), lambda i: (i, 0)),
                  _const_spec(w_down.shape),
                  pl.BlockSpec((tm, d), lambda i: (i, 0)),
                  pl.BlockSpec((1, 1, d), lambda i: (i // tpb, 0, 0)),
                  _const_spec((1, d)), _const_spec((1, d))],
        out_specs=pl.BlockSpec((tm, d), lambda i: (i, 0)),
        out_shape=jax.ShapeDtypeStruct((t, d), F32),
        compiler_params=_params("arbitrary"),
        name="ffn_down_ln2",
    )(hid, w_down, x1, gate2, lg, lb)


def _rope_tables(seq):
    half = ROT_DIM // 2
    pos = jnp.arange(seq, dtype=F32)
    inv_freq = ROPE_THETA ** (-jnp.arange(0, ROT_DIM, 2, dtype=F32) / ROT_DIM)
    ang = pos[:, None] * inv_freq[None, :]
    cos, sin = jnp.cos(ang), jnp.sin(ang)
    ones = jnp.ones((seq, HEAD_DIM - ROT_DIM), F32)
    zeros = jnp.zeros((seq, HEAD_DIM - ROT_DIM), F32)
    zh = jnp.zeros((seq, half), F32)
    tab_c = jnp.concatenate([cos, cos, ones], axis=1)
    tab_a = jnp.concatenate([-sin, zh, zeros], axis=1)
    tab_b = jnp.concatenate([zh, sin, zeros], axis=1)
    tab_s = jnp.concatenate([-sin, sin, zeros], axis=1)
    return (tab_c.T, tab_s.T), (tab_c, tab_a, tab_b)


def kernel(x, c, w_ada, b_ada, w_in, conv_dw_w, conv_dw_b, conv_ln_g, conv_ln_b, w_conv_out, b_conv_out,
           w_attn_out, w_out, ln1_g, ln1_b, w_up, ffn_dw_w, ffn_dw_b, w_down, ln2_g, ln2_b):
    batch, seq, d = x.shape
    ch = conv_dw_w.shape[-1]
    width = N_HEADS * HEAD_DIM
    assert seq % MOBA_BLOCK == 0 and w_ada.shape[0] == DEPTH
    assert 2 * ch % width == 0 and (2 * ch + 3 * width) % (d // 2) == 0
    x2d = x.reshape(batch * seq, d)
    row = lambda v: v.reshape(1, -1)

    c_pad = jnp.pad(c, ((0, SUBLANES - batch), (0, 0)))
    mod = _ada(c_pad, w_ada[0], row(b_ada[0]))[:batch]
    shift1, scale1, gate1, shift2, scale2, gate2 = [m.reshape(batch, 1, d) for m in jnp.split(mod, 6, axis=-1)]

    proj = _inproj(x2d, scale1, shift1, w_in[0].astype(BF16), seq, tm=min(1024, seq), tn=1024)

    hc = _conv_branch(proj, conv_dw_w[0], row(conv_dw_b[0]), row(conv_ln_g[0]), row(conv_ln_b[0]),
                      batch, seq, ch, tm=min(256, seq))

    tabs_t, tabs = _rope_tables(seq)
    qa, ka, vt = _prep(proj, tabs_t, tabs, batch, seq, qcol=2 * ch // width)
    o = _attention(qa, ka, vt, batch, seq, group=min(4, seq // MOBA_BLOCK))

    x1, u2 = _merge(hc, o, proj, x2d, gate1, scale2, shift2,
                    w_conv_out[0].astype(BF16), row(b_conv_out[0]), w_attn_out[0].astype(BF16),
                    w_out[0].astype(BF16), row(ln1_g[0]), row(ln1_b[0]), seq,
                    gcol=(2 * ch + 3 * width) // (d // 2))

    hid = _ffn_up(u2, w_up[0].astype(BF16), ffn_dw_w[0], row(ffn_dw_b[0]), w_down.shape[1], seq,
                  tm=min(1024, seq), tf=512)
    out = _ffn_down(hid, w_down[0].astype(BF16), x1, gate2, row(ln2_g[0]), row(ln2_b[0]), seq, tm=min(512, seq))
    return out.reshape(batch, seq, d)
```

```python
import functools

import jax
import jax.numpy as jnp
from jax import lax
from jax.experimental import pallas as pl
from jax.experimental.pallas import tpu as pltpu

F32 = jnp.float32
BF16 = jnp.bfloat16

N_HEADS = 8
HEAD_DIM = 128
ROT_DIM = HEAD_DIM // 4
ROPE_THETA = 500000.0
MOBA_BLOCK = 256
MOBA_TOPK = 3
CONV_KERNEL = 31
FFN_CONV = 3
LN_EPS = 1e-5
DEPTH = 1
DEEPNORM_ALPHA = (2.0 * DEPTH) ** 0.25

LANES = 128
SUBLANES = 8
MASK_BIAS = -(2.0 ** 30)
LOG2_E = 1.4426950408889634
CONV_HALO = 32
FFN_HALO = SUBLANES
VMEM_LIMIT = 56 * 1024 * 1024


def _params(*sem):
    return pltpu.CompilerParams(dimension_semantics=sem, vmem_limit_bytes=VMEM_LIMIT)


def _const_spec(shape):
    nd = len(shape)
    return pl.BlockSpec(shape, lambda *_: (0,) * nd, pipeline_mode=pl.Buffered(1))


def _layer_norm(z, g, b):
    mu = jnp.mean(z, axis=-1, keepdims=True)
    zc = z - mu
    var = jnp.mean(zc * zc, axis=-1, keepdims=True)
    return zc * lax.rsqrt(var + LN_EPS) * g + b


def _silu(v):
    return v * jax.nn.sigmoid(v)


def _ada_kernel(c_ref, w_ref, b_ref, o_ref):
    ca = _silu(c_ref[...]).astype(BF16)
    o_ref[...] = jnp.dot(ca, w_ref[...].astype(BF16), preferred_element_type=F32) + b_ref[...]


def _ada(c_pad, w_ada, b_ada, tn=1024):
    rows, d = c_pad.shape
    n = w_ada.shape[1]
    return pl.pallas_call(
        _ada_kernel,
        grid=(n // tn,),
        in_specs=[pl.BlockSpec((rows, d), lambda j: (0, 0)),
                  pl.BlockSpec((d, tn), lambda j: (0, j)),
                  pl.BlockSpec((1, tn), lambda j: (0, j))],
        out_specs=pl.BlockSpec((rows, tn), lambda j: (0, j)),
        out_shape=jax.ShapeDtypeStruct((rows, n), F32),
        compiler_params=_params("arbitrary"),
        name="ada",
    )(c_pad, w_ada, b_ada)


def _inproj_kernel(x_ref, sc_ref, sh_ref, w_ref, o_ref, u_ref):
    @pl.when(pl.program_id(1) == 0)
    def _():
        u_ref[...] = (x_ref[...] * (1.0 + sc_ref[0]) + sh_ref[0]).astype(BF16)

    o_ref[...] = jnp.dot(u_ref[...], w_ref[...], preferred_element_type=F32)


def _inproj(x2d, scale, shift, w, seq, tm, tn):
    t, d = x2d.shape
    n = w.shape[1]
    tpb = seq // tm
    return pl.pallas_call(
        _inproj_kernel,
        grid=(t // tm, n // tn),
        in_specs=[pl.BlockSpec((tm, d), lambda i, j: (i, 0)),
                  pl.BlockSpec((1, 1, d), lambda i, j: (i // tpb, 0, 0)),
                  pl.BlockSpec((1, 1, d), lambda i, j: (i // tpb, 0, 0)),
                  pl.BlockSpec((d, tn), lambda i, j: (0, j))],
        out_specs=pl.BlockSpec((tm, tn), lambda i, j: (i, j)),
        out_shape=jax.ShapeDtypeStruct((t, n), F32),
        scratch_shapes=[pltpu.VMEM((tm, d), BF16)],
        compiler_params=_params("arbitrary", "arbitrary"),
        name="inproj",
    )(x2d, scale, shift, w)


def _conv_kernel(al_ref, ag_ref, w_ref, b_ref, g_ref, be_ref, o_ref, win_ref, wb_ref, y_ref, *,
                 tm, rows, lanes, ln_rows):
    i = pl.program_id(1)

    ch = win_ref.shape[2]

    @pl.when(jnp.logical_and(pl.program_id(0) == 0, i == 0))
    def _():
        for j in range(CONV_KERNEL):
            wb_ref[j] = jnp.broadcast_to(w_ref[j:j + 1, :], (SUBLANES, ch))
        wb_ref[CONV_KERNEL] = jnp.broadcast_to(b_ref[...], (SUBLANES, ch))

    @pl.when(i == 0)
    def _():
        win_ref[0, 0:CONV_HALO, :] = jnp.zeros((CONV_HALO, ch), F32)

    @pl.when(i > 0)
    def _():
        win_ref[0, 0:CONV_HALO, :] = win_ref[0, tm:tm + CONV_HALO, :]

    win_ref[0, CONV_HALO:CONV_HALO + tm, :] = al_ref[...] * jax.nn.sigmoid(ag_ref[...])

    n_shift = tm + CONV_HALO - SUBLANES
    for s in range(1, SUBLANES):
        win_ref[s, 0:n_shift, :] = win_ref[0, s:s + n_shift, :]

    first = CONV_HALO - (CONV_KERNEL - 1)

    def body(s, carry):
        r0 = pl.multiple_of(s * rows, rows)
        for c0 in range(0, ch, lanes):
            cs = slice(c0, c0 + lanes)
            accs = [wb_ref[CONV_KERNEL, :, cs]] * (rows // SUBLANES)
            for j in range(CONV_KERNEL):
                tiles, rem = divmod(first + j, SUBLANES)
                wj = wb_ref[j, :, cs]
                for k in range(rows // SUBLANES):
                    start = pl.multiple_of(r0 + (tiles + k) * SUBLANES, SUBLANES)
                    accs[k] = accs[k] + win_ref[rem, pl.ds(start, SUBLANES), cs] * wj
            y_ref[pl.ds(r0, rows), cs] = jnp.concatenate(accs, axis=0)
        for rr in range(0, rows, ln_rows):
            y = y_ref[pl.ds(pl.multiple_of(r0 + rr, ln_rows), ln_rows), :]
            y = _silu(_layer_norm(y, g_ref[...], be_ref[...]))
            o_ref[pl.ds(pl.multiple_of(r0 + rr, ln_rows), ln_rows), :] = y.astype(o_ref.dtype)
        return carry

    lax.fori_loop(0, tm // rows, body, 0)


def _conv_branch(proj, dw_w, dw_b, ln_g, ln_b, batch, seq, ch, tm=256, rows=128, lanes=256, ln_rows=32):
    t = proj.shape[0]
    tpb = seq // tm
    kern = functools.partial(_conv_kernel, tm=tm, rows=rows, lanes=lanes, ln_rows=ln_rows)
    return pl.pallas_call(
        kern,
        grid=(batch, tpb),
        in_specs=[pl.BlockSpec((tm, ch), lambda b, i: (b * tpb + i, 0)),
                  pl.BlockSpec((tm, ch), lambda b, i: (b * tpb + i, 1)),
                  _const_spec(dw_w.shape), _const_spec((1, ch)), _const_spec((1, ch)), _const_spec((1, ch))],
        out_specs=pl.BlockSpec((tm, ch), lambda b, i: (b * tpb + i, 0)),
        out_shape=jax.ShapeDtypeStruct((t, ch), BF16),
        scratch_shapes=[pltpu.VMEM((SUBLANES, tm + CONV_HALO, ch), F32),
                        pltpu.VMEM((CONV_KERNEL + 1, SUBLANES, ch), F32),
                        pltpu.VMEM((tm, ch), F32)],
        compiler_params=_params("arbitrary", "arbitrary"),
        name="conv_branch",
    )(proj, proj, dw_w, dw_b, ln_g, ln_b)


def _prep_kernel(q_ref, k_ref, v_ref, ct_ref, st_ref, c_ref, a_ref, b_ref, qa_ref, ka_ref, vt_ref, km_ref):
    n = pl.program_id(1)
    blk = q_ref.shape[0]
    half = ROT_DIM // 2

    @pl.when(n == 0)
    def _():
        km_ref[...] = jnp.zeros(km_ref.shape, F32)

    cc, aa, bb = c_ref[...], a_ref[...], b_ref[...]
    ct, st = ct_ref[...], st_ref[...]

    def rope(xh):
        return xh * cc + pltpu.roll(xh, LANES - half, 1) * aa + pltpu.roll(xh, half, 1) * bb

    def rope_t(xt):
        partner = jnp.concatenate([xt[half:ROT_DIM], xt[:half], xt[ROT_DIM:]], axis=0)
        return xt * ct + partner * st

    lane = lax.broadcasted_iota(jnp.int32, (blk, LANES), 1)
    onehot_n = jnp.where(lane == n, 1.0, 0.0).astype(BF16)
    ids = lax.broadcasted_iota(jnp.int32, (LANES, blk), 0).astype(F32)
    n_past = jnp.full((LANES, blk), n.astype(F32))
    km_row = lax.broadcasted_iota(jnp.int32, (km_ref.shape[0], HEAD_DIM), 0)
    scale = HEAD_DIM ** -0.5 * LOG2_E
    aug = HEAD_DIM + LANES

    for h in range(N_HEADS):
        sl = slice(h * HEAD_DIM, (h + 1) * HEAD_DIM)
        qt = rope_t(q_ref[:, sl].T)
        kr = rope(k_ref[:, sl])
        km = km_ref[:, sl]

        gate = jnp.dot(km.astype(BF16), qt.astype(BF16), preferred_element_type=F32)
        g = jnp.where(ids < n_past, gate, -jnp.inf)
        sel = jnp.zeros(gate.shape, jnp.bool_)
        for r in range(MOBA_TOPK):
            mx = jnp.max(g, axis=0, keepdims=True)
            first = jnp.min(jnp.where(g == mx, ids, float(LANES)), axis=0, keepdims=True)
            pick = ids == first
            sel = jnp.logical_or(sel, jnp.logical_and(pick, n_past > float(r)))
            g = jnp.where(pick, -jnp.inf, g)

        qa_ref[h * aug:h * aug + HEAD_DIM, :] = (qt * scale).astype(BF16)
        qa_ref[h * aug + HEAD_DIM:(h + 1) * aug, :] = jnp.where(sel, 0.0, MASK_BIAS).astype(BF16)
        ka_ref[:, h * aug:h * aug + HEAD_DIM] = kr.astype(BF16)
        ka_ref[:, h * aug + HEAD_DIM:(h + 1) * aug] = onehot_n
        km_ref[:, sl] = jnp.where(km_row == n, jnp.mean(kr, axis=0, keepdims=True), km)
    vt_ref[...] = v_ref[...].T.astype(BF16)


def _prep(proj, tabs_t, tabs, batch, seq, qcol):
    t = proj.shape[0]
    w = N_HEADS * HEAD_DIM
    nb = seq // MOBA_BLOCK
    assert nb <= LANES
    blk = lambda c: pl.BlockSpec((MOBA_BLOCK, w), lambda b, n: (b * nb + n, c))
    tab = pl.BlockSpec((MOBA_BLOCK, HEAD_DIM), lambda b, n: (n, 0))
    tab_t = pl.BlockSpec((HEAD_DIM, MOBA_BLOCK), lambda b, n: (0, n))
    return pl.pallas_call(
        _prep_kernel,
        grid=(batch, nb),
        in_specs=[blk(qcol), blk(qcol + 1), blk(qcol + 2), tab_t, tab_t, tab, tab, tab],
        out_specs=[pl.BlockSpec((2 * w, MOBA_BLOCK), lambda b, n: (b, n)),
                   pl.BlockSpec((MOBA_BLOCK, 2 * w), lambda b, n: (b * nb + n, 0)),
                   pl.BlockSpec((w, MOBA_BLOCK), lambda b, n: (b, n))],
        out_shape=[jax.ShapeDtypeStruct((batch * 2 * w, seq), BF16), jax.ShapeDtypeStruct((t, 2 * w), BF16),
                   jax.ShapeDtypeStruct((batch * w, seq), BF16)],
        scratch_shapes=[pltpu.VMEM((LANES, w), F32)],
        compiler_params=_params("arbitrary", "arbitrary"),
        name="rope_prep",
    )(proj, proj, proj, *tabs_t, *tabs)


def _attn_kernel(qa_ref, ka_ref, vt_ref, o_ref, s_ref, *, group, heads):
    blk = MOBA_BLOCK
    tq = o_ref.shape[0]
    qb0 = pl.program_id(2) * (tq // blk)
    aug = HEAD_DIM + LANES
    start = pl.multiple_of(qb0 * blk, tq)
    key = lax.broadcasted_iota(jnp.int32, (tq, tq), 0)
    qry = lax.broadcasted_iota(jnp.int32, (tq, tq), 1)
    sh = blk.bit_length() - 1
    own = jnp.logical_and(key <= qry, lax.shift_right_logical(key, sh) == lax.shift_right_logical(qry, sh))

    def init(h):
        q_t = qa_ref[h * aug:h * aug + HEAD_DIM, :]
        k_own = ka_ref[pl.ds(start, tq), h * aug:h * aug + HEAD_DIM]
        s = jnp.dot(k_own, q_t, preferred_element_type=F32)
        s = jnp.where(own, s, -jnp.inf)
        m = jnp.max(s, axis=0, keepdims=True)
        p = jnp.exp2(s - m)
        l = jnp.sum(p, axis=0, keepdims=True)
        v_own = vt_ref[h * HEAD_DIM:(h + 1) * HEAD_DIM, pl.ds(start, tq)]
        return m, l, jnp.dot(v_own, p.astype(BF16), preferred_element_type=F32)

    span = group * blk
    last_off = ka_ref.shape[0] - span

    def logits_to(buf, h, g):
        off = pl.multiple_of(jnp.minimum(g * span, last_off), span)
        k_grp = ka_ref[pl.ds(off, span), h * aug:(h + 1) * aug]
        s_ref[buf, h] = jnp.dot(k_grp, qa_ref[h * aug:(h + 1) * aug, :], preferred_element_type=F32)

    def update(buf, h, g, state):
        m, l, acc = state
        off = pl.multiple_of(g * span, span)
        m_new = jnp.maximum(m, jnp.max(s_ref[buf, h], axis=0, keepdims=True))
        p = jnp.exp2(s_ref[buf, h] - m_new)
        corr = jnp.exp2(m - m_new)
        l_new = corr * l + jnp.sum(p, axis=0, keepdims=True)
        v_grp = vt_ref[h * HEAD_DIM:(h + 1) * HEAD_DIM, pl.ds(off, span)]
        return m_new, l_new, corr * acc + jnp.dot(v_grp, p.astype(BF16), preferred_element_type=F32)

    def body(t, states):
        for h in range(heads):
            logits_to(1, h, 2 * t + 1)
        states = tuple(update(0, h, 2 * t, states[h]) for h in range(heads))
        for h in range(heads):
            logits_to(0, h, 2 * t + 2)
        return tuple(update(1, h, 2 * t + 1, states[h]) for h in range(heads))

    n_swept = qb0 + tq // blk - 1
    n_trips = (n_swept + 2 * group - 1) // (2 * group)
    for h in range(heads):
        logits_to(0, h, 0)
    states = lax.fori_loop(0, n_trips, body, tuple(init(h) for h in range(heads)))
    for h in range(heads):
        m, l, acc = states[h]
        o_ref[:, h * HEAD_DIM:(h + 1) * HEAD_DIM] = (acc / l).T.astype(o_ref.dtype)


def _attention(qa, ka, vt, batch, seq, group=4, heads=2, tq=2 * MOBA_BLOCK):
    t = ka.shape[0]
    w = N_HEADS * HEAD_DIM
    nb = seq // MOBA_BLOCK
    aug = HEAD_DIM + LANES
    assert nb % (2 * group) == 0 and N_HEADS % heads == 0 and seq % tq == 0 and tq % MOBA_BLOCK == 0
    hg = N_HEADS // heads
    nt = seq // tq
    kern = functools.partial(_attn_kernel, group=group, heads=heads)
    return pl.pallas_call(
        kern,
        grid=(batch, hg, nt),
        in_specs=[pl.BlockSpec((heads * aug, tq), lambda b, h, n: (b * hg + h, n)),
                  pl.BlockSpec((seq, heads * aug), lambda b, h, n: (b, h), pipeline_mode=pl.Buffered(1)),
                  pl.BlockSpec((heads * HEAD_DIM, seq), lambda b, h, n: (b * hg + h, 0),
                               pipeline_mode=pl.Buffered(1))],
        out_specs=pl.BlockSpec((tq, heads * HEAD_DIM), lambda b, h, n: (b * nt + n, h)),
        out_shape=jax.ShapeDtypeStruct((t, w), BF16),
        scratch_shapes=[pltpu.VMEM((2, heads, group * MOBA_BLOCK, tq), F32)],
        compiler_params=_params("arbitrary", "arbitrary", "arbitrary"),
        name="moba_attn",
    )(qa, ka, vt)


def _merge_kernel(hc_ref, o_ref, gc0_ref, gc1_ref, ga0_ref, ga1_ref, x_ref, g1_ref, sc2_ref, sh2_ref,
                  wc_ref, bc_ref, wa_ref, wo_ref, lg_ref, lb_ref, x1_ref, u2_ref, *, rc):
    for r0 in range(0, x_ref.shape[0], rc):
        rows = slice(r0, r0 + rc)
        yc = jnp.dot(hc_ref[rows, :], wc_ref[...], preferred_element_type=F32) + bc_ref[...]
        ya = jnp.dot(o_ref[rows, :], wa_ref[...], preferred_element_type=F32)
        gc = jax.nn.sigmoid(jnp.concatenate([gc0_ref[rows, :], gc1_ref[rows, :]], axis=1))
        ga = jax.nn.sigmoid(jnp.concatenate([ga0_ref[rows, :], ga1_ref[rows, :]], axis=1))
        mix = (gc * yc + ga * ya).astype(BF16)
        y = jnp.dot(mix, wo_ref[...], preferred_element_type=F32)
        z = DEEPNORM_ALPHA * x_ref[rows, :] + (1.0 + g1_ref[0]) * y
        x1 = _layer_norm(z, lg_ref[...], lb_ref[...])
        x1_ref[rows, :] = x1
        u2_ref[rows, :] = (x1 * (1.0 + sc2_ref[0]) + sh2_ref[0]).astype(BF16)


def _merge(hc, o, proj, x2d, gate1, scale2, shift2, wc, bc, wa, wo, lg, lb, seq, gcol, tm=256, rc=256):
    t, d = x2d.shape
    ch = hc.shape[1]
    w = o.shape[1]
    tpb = seq // tm
    row = lambda cols, c: pl.BlockSpec((tm, cols), lambda i: (i, c))
    mod = pl.BlockSpec((1, 1, d), lambda i: (i // tpb, 0, 0))
    return pl.pallas_call(
        functools.partial(_merge_kernel, rc=min(rc, tm)),
        grid=(t // tm,),
        in_specs=[row(ch, 0), row(w, 0),
                  row(d // 2, gcol), row(d // 2, gcol + 1), row(d // 2, gcol + 2), row(d // 2, gcol + 3),
                  row(d, 0), mod, mod, mod,
                  _const_spec(wc.shape), _const_spec((1, d)), _const_spec(wa.shape), _const_spec(wo.shape),
                  _const_spec((1, d)), _const_spec((1, d))],
        out_specs=[row(d, 0), row(d, 0)],
        out_shape=[jax.ShapeDtypeStruct((t, d), F32), jax.ShapeDtypeStruct((t, d), BF16)],
        compiler_params=_params("arbitrary"),
        name="merge_ln1",
    )(hc, o, proj, proj, proj, proj, x2d, gate1, scale2, shift2, wc, bc, wa, wo, lg, lb)


def _ffn_up_kernel(u_ref, wa_ref, wv_ref, cw_ref, cb_ref, h_ref, abuf_ref, carry_ref, *, tm, tpb, rc):
    i = pl.program_id(0)
    f = pl.program_id(1)

    first_of_batch = (i % tpb) == 0

    @pl.when(first_of_batch)
    def _():
        abuf_ref[0:FFN_HALO, :] = jnp.zeros((FFN_HALO, abuf_ref.shape[1]), F32)

    @pl.when(jnp.logical_not(first_of_batch))
    def _():
        abuf_ref[0:FFN_HALO, :] = carry_ref[f]

    for r0 in range(0, tm, rc):
        u = u_ref[r0:r0 + rc, :]
        a = jnp.dot(u, wa_ref[...], preferred_element_type=F32)
        val = jnp.dot(u, wv_ref[...], preferred_element_type=F32)
        abuf_ref[FFN_HALO + r0:FFN_HALO + r0 + rc, :] = a
        conv = a * cw_ref[FFN_CONV - 1:FFN_CONV, :] + cb_ref[...]
        for j in range(FFN_CONV - 1):
            start = FFN_HALO + r0 - (FFN_CONV - 1 - j)
            conv = conv + abuf_ref[start:start + rc, :] * cw_ref[j:j + 1, :]
        h_ref[r0:r0 + rc, :] = (_silu(conv) * val).astype(h_ref.dtype)
    carry_ref[f] = abuf_ref[tm:tm + FFN_HALO, :]


def _ffn_up(u2, w_up, cw, cb, dff, seq, tm, tf, rc=256):
    t, d = u2.shape
    nf = dff // tf
    tpb = seq // tm
    kern = functools.partial(_ffn_up_kernel, tm=tm, tpb=tpb, rc=min(rc, tm))
    return pl.pallas_call(
        kern,
        grid=(t // tm, nf),
        in_specs=[pl.BlockSpec((tm, d), lambda i, f: (i, 0)),
                  pl.BlockSpec((d, tf), lambda i, f: (0, f)),
                  pl.BlockSpec((d, tf), lambda i, f: (0, nf + f)),
                  pl.BlockSpec((FFN_CONV, tf), lambda i, f: (0, f)),
                  pl.BlockSpec((1, tf), lambda i, f: (0, f))],
        out_specs=pl.BlockSpec((tm, tf), lambda i, f: (i, f)),
        out_shape=jax.ShapeDtypeStruct((t, dff), BF16),
        scratch_shapes=[pltpu.VMEM((tm + FFN_HALO, tf), F32),
                        pltpu.VMEM((nf, FFN_HALO, tf), F32)],
        compiler_params=_params("arbitrary", "arbitrary"),
        name="convffn_up",
    )(u2, w_up, w_up, cw, cb)


def _ffn_down_kernel(h_ref, wd_ref, x1_ref, g2_ref, lg_ref, lb_ref, o_ref, *, rc):
    for r0 in range(0, h_ref.shape[0], rc):
        y = jnp.dot(h_ref[r0:r0 + rc, :], wd_ref[...], preferred_element_type=F32)
        z = DEEPNORM_ALPHA * x1_ref[r0:r0 + rc, :] + (1.0 + g2_ref[0]) * y
        o_ref[r0:r0 + rc, :] = _layer_norm(z, lg_ref[...], lb_ref[...])


def _ffn_down(hid, w_down, x1, gate2, lg, lb, seq, tm, rc=256):
    t, dff = hid.shape
    d = w_down.shape[1]
    tpb = seq // tm
    return pl.pallas_call(
        functools.partial(_ffn_down_kernel, rc=min(rc, tm)),
        grid=(t // tm,),
        in_specs=[pl.BlockSpec((tm, dff), lambda i: (i, 0)),
                  _const_spec(w_down.shape),
                  pl.BlockSpec((tm, d), lambda i: (i, 0)),
                  pl.BlockSpec((1, 1, d), lambda i: (i // tpb, 0, 0)),
                  _const_spec((1, d)), _const_spec((1, d))],
        out_specs=pl.BlockSpec((tm, d), lambda i: (i, 0)),
        out_shape=jax.ShapeDtypeStruct((t, d), F32),
        compiler_params=_params("arbitrary"),
        name="ffn_down_ln2",
    )(hid, w_down, x1, gate2, lg, lb)


def _rope_tables(seq):
    half = ROT_DIM // 2
    pos = jnp.arange(seq, dtype=F32)
    inv_freq = ROPE_THETA ** (-jnp.arange(0, ROT_DIM, 2, dtype=F32) / ROT_DIM)
    ang = pos[:, None] * inv_freq[None, :]
    cos, sin = jnp.cos(ang), jnp.sin(ang)
    ones = jnp.ones((seq, HEAD_DIM - ROT_DIM), F32)
    zeros = jnp.zeros((seq, HEAD_DIM - ROT_DIM), F32)
    zh = jnp.zeros((seq, half), F32)
    tab_c = jnp.concatenate([cos, cos, ones], axis=1)
    tab_a = jnp.concatenate([-sin, zh, zeros], axis=1)
    tab_b = jnp.concatenate([zh, sin, zeros], axis=1)
    tab_s = jnp.concatenate([-sin, sin, zeros], axis=1)
    return (tab_c.T, tab_s.T), (tab_c, tab_a, tab_b)


def kernel(x, c, w_ada, b_ada, w_in, conv_dw_w, conv_dw_b, conv_ln_g, conv_ln_b, w_conv_out, b_conv_out,
           w_attn_out, w_out, ln1_g, ln1_b, w_up, ffn_dw_w, ffn_dw_b, w_down, ln2_g, ln2_b):
    batch, seq, d = x.shape
    ch = conv_dw_w.shape[-1]
    width = N_HEADS * HEAD_DIM
    assert seq % MOBA_BLOCK == 0 and w_ada.shape[0] == DEPTH
    assert 2 * ch % width == 0 and (2 * ch + 3 * width) % (d // 2) == 0
    x2d = x.reshape(batch * seq, d)
    row = lambda v: v.reshape(1, -1)

    c_pad = jnp.pad(c, ((0, SUBLANES - batch), (0, 0)))
    mod = _ada(c_pad, w_ada[0], row(b_ada[0]))[:batch]
    shift1, scale1, gate1, shift2, scale2, gate2 = [m.reshape(batch, 1, d) for m in jnp.split(mod, 6, axis=-1)]

    proj = _inproj(x2d, scale1, shift1, w_in[0].astype(BF16), seq, tm=min(1024, seq), tn=1024)

    hc = _conv_branch(proj, conv_dw_w[0], row(conv_dw_b[0]), row(conv_ln_g[0]), row(conv_ln_b[0]),
                      batch, seq, ch, tm=min(256, seq))

    tabs_t, tabs = _rope_tables(seq)
    qa, ka, vt = _prep(proj, tabs_t, tabs, batch, seq, qcol=2 * ch // width)
    o = _attention(qa, ka, vt, batch, seq, group=min(4, seq // MOBA_BLOCK // 2))

    x1, u2 = _merge(hc, o, proj, x2d, gate1, scale2, shift2,
                    w_conv_out[0].astype(BF16), row(b_conv_out[0]), w_attn_out[0].astype(BF16),
                    w_out[0].astype(BF16), row(ln1_g[0]), row(ln1_b[0]), seq,
                    gcol=(2 * ch + 3 * width) // (d // 2))

    hid = _ffn_up(u2, w_up[0].astype(BF16), ffn_dw_w[0], row(ffn_dw_b[0]), w_down.shape[1], seq,
                  tm=min(1024, seq), tf=512)
    out = _ffn_down(hid, w_down[0].astype(BF16), x1, gate2, row(ln2_g[0]), row(ln2_b[0]), seq, tm=min(512, seq))
    return out.reshape(batch, seq, d)
```

```python
import functools

import jax
import jax.numpy as jnp
from jax import lax
from jax.experimental import pallas as pl
from jax.experimental.pallas import tpu as pltpu

F32 = jnp.float32
BF16 = jnp.bfloat16

N_HEADS = 8
HEAD_DIM = 128
ROT_DIM = HEAD_DIM // 4
ROPE_THETA = 500000.0
MOBA_BLOCK = 256
MOBA_TOPK = 3
CONV_KERNEL = 31
FFN_CONV = 3
LN_EPS = 1e-5
DEPTH = 1
DEEPNORM_ALPHA = (2.0 * DEPTH) ** 0.25

LANES = 128
SUBLANES = 8
MASK_BIAS = -(2.0 ** 30)
LOG2_E = 1.4426950408889634
CONV_HALO = 32
FFN_HALO = SUBLANES
VMEM_LIMIT = 56 * 1024 * 1024


def _params(*sem):
    return pltpu.CompilerParams(dimension_semantics=sem, vmem_limit_bytes=VMEM_LIMIT)


def _const_spec(shape):
    nd = len(shape)
    return pl.BlockSpec(shape, lambda *_: (0,) * nd, pipeline_mode=pl.Buffered(1))


def _layer_norm(z, g, b):
    mu = jnp.mean(z, axis=-1, keepdims=True)
    zc = z - mu
    var = jnp.mean(zc * zc, axis=-1, keepdims=True)
    return zc * lax.rsqrt(var + LN_EPS) * g + b


def _silu(v):
    return v * jax.nn.sigmoid(v)


def _ada_kernel(c_ref, w_ref, b_ref, o_ref):
    ca = _silu(c_ref[...]).astype(BF16)
    o_ref[...] = jnp.dot(ca, w_ref[...].astype(BF16), preferred_element_type=F32) + b_ref[...]


def _ada(c_pad, w_ada, b_ada, tn=1024):
    rows, d = c_pad.shape
    n = w_ada.shape[1]
    return pl.pallas_call(
        _ada_kernel,
        grid=(n // tn,),
        in_specs=[pl.BlockSpec((rows, d), lambda j: (0, 0)),
                  pl.BlockSpec((d, tn), lambda j: (0, j)),
                  pl.BlockSpec((1, tn), lambda j: (0, j))],
        out_specs=pl.BlockSpec((rows, tn), lambda j: (0, j)),
        out_shape=jax.ShapeDtypeStruct((rows, n), F32),
        compiler_params=_params("arbitrary"),
        name="ada",
    )(c_pad, w_ada, b_ada)


def _inproj_kernel(x_ref, sc_ref, sh_ref, w_ref, o_ref, u_ref):
    @pl.when(pl.program_id(1) == 0)
    def _():
        u_ref[...] = (x_ref[...] * (1.0 + sc_ref[0]) + sh_ref[0]).astype(BF16)

    o_ref[...] = jnp.dot(u_ref[...], w_ref[...], preferred_element_type=F32)


def _inproj(x2d, scale, shift, w, seq, tm, tn):
    t, d = x2d.shape
    n = w.shape[1]
    tpb = seq // tm
    return pl.pallas_call(
        _inproj_kernel,
        grid=(t // tm, n // tn),
        in_specs=[pl.BlockSpec((tm, d), lambda i, j: (i, 0)),
                  pl.BlockSpec((1, 1, d), lambda i, j: (i // tpb, 0, 0)),
                  pl.BlockSpec((1, 1, d), lambda i, j: (i // tpb, 0, 0)),
                  pl.BlockSpec((d, tn), lambda i, j: (0, j))],
        out_specs=pl.BlockSpec((tm, tn), lambda i, j: (i, j)),
        out_shape=jax.ShapeDtypeStruct((t, n), F32),
        scratch_shapes=[pltpu.VMEM((tm, d), BF16)],
        compiler_params=_params("arbitrary", "arbitrary"),
        name="inproj",
    )(x2d, scale, shift, w)


def _conv_kernel(al_ref, ag_ref, w_ref, b_ref, g_ref, be_ref, o_ref, win_ref, wb_ref, y_ref, *,
                 tm, rows, lanes, ln_rows):
    i = pl.program_id(1)

    ch = win_ref.shape[2]

    @pl.when(jnp.logical_and(pl.program_id(0) == 0, i == 0))
    def _():
        for j in range(CONV_KERNEL):
            wb_ref[j] = jnp.broadcast_to(w_ref[j:j + 1, :], (SUBLANES, ch))
        wb_ref[CONV_KERNEL] = jnp.broadcast_to(b_ref[...], (SUBLANES, ch))

    @pl.when(i == 0)
    def _():
        win_ref[0, 0:CONV_HALO, :] = jnp.zeros((CONV_HALO, ch), F32)

    @pl.when(i > 0)
    def _():
        win_ref[0, 0:CONV_HALO, :] = win_ref[0, tm:tm + CONV_HALO, :]

    win_ref[0, CONV_HALO:CONV_HALO + tm, :] = al_ref[...] * jax.nn.sigmoid(ag_ref[...])

    n_shift = tm + CONV_HALO - SUBLANES
    for s in range(1, SUBLANES):
        win_ref[s, 0:n_shift, :] = win_ref[0, s:s + n_shift, :]

    first = CONV_HALO - (CONV_KERNEL - 1)

    def body(s, carry):
        r0 = pl.multiple_of(s * rows, rows)
        for c0 in range(0, ch, lanes):
            cs = slice(c0, c0 + lanes)
            accs = [wb_ref[CONV_KERNEL, :, cs]] * (rows // SUBLANES)
            for j in range(CONV_KERNEL):
                tiles, rem = divmod(first + j, SUBLANES)
                wj = wb_ref[j, :, cs]
                for k in range(rows // SUBLANES):
                    start = pl.multiple_of(r0 + (tiles + k) * SUBLANES, SUBLANES)
                    accs[k] = accs[k] + win_ref[rem, pl.ds(start, SUBLANES), cs] * wj
            y_ref[pl.ds(r0, rows), cs] = jnp.concatenate(accs, axis=0)
        for rr in range(0, rows, ln_rows):
            y = y_ref[pl.ds(pl.multiple_of(r0 + rr, ln_rows), ln_rows), :]
            y = _silu(_layer_norm(y, g_ref[...], be_ref[...]))
            o_ref[pl.ds(pl.multiple_of(r0 + rr, ln_rows), ln_rows), :] = y.astype(o_ref.dtype)
        return carry

    lax.fori_loop(0, tm // rows, body, 0)


def _conv_branch(proj, dw_w, dw_b, ln_g, ln_b, batch, seq, ch, tm=256, rows=128, lanes=256, ln_rows=32):
    t = proj.shape[0]
    tpb = seq // tm
    kern = functools.partial(_conv_kernel, tm=tm, rows=rows, lanes=lanes, ln_rows=ln_rows)
    return pl.pallas_call(
        kern,
        grid=(batch, tpb),
        in_specs=[pl.BlockSpec((tm, ch), lambda b, i: (b * tpb + i, 0)),
                  pl.BlockSpec((tm, ch), lambda b, i: (b * tpb + i, 1)),
                  _const_spec(dw_w.shape), _const_spec((1, ch)), _const_spec((1, ch)), _const_spec((1, ch))],
        out_specs=pl.BlockSpec((tm, ch), lambda b, i: (b * tpb + i, 0)),
        out_shape=jax.ShapeDtypeStruct((t, ch), BF16),
        scratch_shapes=[pltpu.VMEM((SUBLANES, tm + CONV_HALO, ch), F32),
                        pltpu.VMEM((CONV_KERNEL + 1, SUBLANES, ch), F32),
                        pltpu.VMEM((tm, ch), F32)],
        compiler_params=_params("arbitrary", "arbitrary"),
        name="conv_branch",
    )(proj, proj, dw_w, dw_b, ln_g, ln_b)


def _prep_kernel(q_ref, k_ref, v_ref, ct_ref, st_ref, c_ref, a_ref, b_ref, qa_ref, ka_ref, vt_ref, km_ref):
    n = pl.program_id(1)
    blk = q_ref.shape[0]
    half = ROT_DIM // 2

    @pl.when(n == 0)
    def _():
        km_ref[...] = jnp.zeros(km_ref.shape, F32)

    cc, aa, bb = c_ref[...], a_ref[...], b_ref[...]
    ct, st = ct_ref[...], st_ref[...]

    def rope(xh):
        return xh * cc + pltpu.roll(xh, LANES - half, 1) * aa + pltpu.roll(xh, half, 1) * bb

    def rope_t(xt):
        partner = jnp.concatenate([xt[half:ROT_DIM], xt[:half], xt[ROT_DIM:]], axis=0)
        return xt * ct + partner * st

    lane = lax.broadcasted_iota(jnp.int32, (blk, LANES), 1)
    onehot_n = jnp.where(lane == n, 1.0, 0.0).astype(BF16)
    ids = lax.broadcasted_iota(jnp.int32, (LANES, blk), 0).astype(F32)
    n_past = jnp.full((LANES, blk), n.astype(F32))
    km_row = lax.broadcasted_iota(jnp.int32, (km_ref.shape[0], HEAD_DIM), 0)
    scale = HEAD_DIM ** -0.5 * LOG2_E
    aug = HEAD_DIM + LANES

    for h in range(N_HEADS):
        sl = slice(h * HEAD_DIM, (h + 1) * HEAD_DIM)
        qt = rope_t(q_ref[:, sl].T)
        kr = rope(k_ref[:, sl])
        km = km_ref[:, sl]

        gate = jnp.dot(km.astype(BF16), qt.astype(BF16), preferred_element_type=F32)
        g = jnp.where(ids < n_past, gate, -jnp.inf)
        sel = jnp.zeros(gate.shape, jnp.bool_)
        for r in range(MOBA_TOPK):
            mx = jnp.max(g, axis=0, keepdims=True)
            first = jnp.min(jnp.where(g == mx, ids, float(LANES)), axis=0, keepdims=True)
            pick = ids == first
            sel = jnp.logical_or(sel, jnp.logical_and(pick, n_past > float(r)))
            g = jnp.where(pick, -jnp.inf, g)

        qa_ref[h * aug:h * aug + HEAD_DIM, :] = (qt * scale).astype(BF16)
        qa_ref[h * aug + HEAD_DIM:(h + 1) * aug, :] = jnp.where(sel, 0.0, MASK_BIAS).astype(BF16)
        ka_ref[:, h * aug:h * aug + HEAD_DIM] = kr.astype(BF16)
        ka_ref[:, h * aug + HEAD_DIM:(h + 1) * aug] = onehot_n
        km_ref[:, sl] = jnp.where(km_row == n, jnp.mean(kr, axis=0, keepdims=True), km)
    vt_ref[...] = v_ref[...].T.astype(BF16)


def _prep(proj, tabs_t, tabs, batch, seq, qcol):
    t = proj.shape[0]
    w = N_HEADS * HEAD_DIM
    nb = seq // MOBA_BLOCK
    assert nb <= LANES
    blk = lambda c: pl.BlockSpec((MOBA_BLOCK, w), lambda b, n: (b * nb + n, c))
    tab = pl.BlockSpec((MOBA_BLOCK, HEAD_DIM), lambda b, n: (n, 0))
    tab_t = pl.BlockSpec((HEAD_DIM, MOBA_BLOCK), lambda b, n: (0, n))
    return pl.pallas_call(
        _prep_kernel,
        grid=(batch, nb),
        in_specs=[blk(qcol), blk(qcol + 1), blk(qcol + 2), tab_t, tab_t, tab, tab, tab],
        out_specs=[pl.BlockSpec((2 * w, MOBA_BLOCK), lambda b, n: (b, n)),
                   pl.BlockSpec((MOBA_BLOCK, 2 * w), lambda b, n: (b * nb + n, 0)),
                   pl.BlockSpec((w, MOBA_BLOCK), lambda b, n: (b, n))],
        out_shape=[jax.ShapeDtypeStruct((batch * 2 * w, seq), BF16), jax.ShapeDtypeStruct((t, 2 * w), BF16),
                   jax.ShapeDtypeStruct((batch * w, seq), BF16)],
        scratch_shapes=[pltpu.VMEM((LANES, w), F32)],
        compiler_params=_params("arbitrary", "arbitrary"),
        name="rope_prep",
    )(proj, proj, proj, *tabs_t, *tabs)


def _attn_kernel(qa_ref, ka_ref, vt_ref, o_ref, s_ref, *, group, heads):
    blk = MOBA_BLOCK
    tq = o_ref.shape[0]
    qb0 = pl.program_id(2) * (tq // blk)
    aug = HEAD_DIM + LANES
    start = pl.multiple_of(qb0 * blk, tq)
    key = lax.broadcasted_iota(jnp.int32, (tq, tq), 0)
    qry = lax.broadcasted_iota(jnp.int32, (tq, tq), 1)
    sh = blk.bit_length() - 1
    own = jnp.logical_and(key <= qry, lax.shift_right_logical(key, sh) == lax.shift_right_logical(qry, sh))

    def init(h):
        q_t = qa_ref[h * aug:h * aug + HEAD_DIM, :]
        k_own = ka_ref[pl.ds(start, tq), h * aug:h * aug + HEAD_DIM]
        s = jnp.dot(k_own, q_t, preferred_element_type=F32)
        s = jnp.where(own, s, -jnp.inf)
        m = jnp.max(s, axis=0, keepdims=True)
        p = jnp.exp2(s - m)
        l = jnp.sum(p, axis=0, keepdims=True)
        v_own = vt_ref[h * HEAD_DIM:(h + 1) * HEAD_DIM, pl.ds(start, tq)]
        return m, l, jnp.dot(v_own, p.astype(BF16), preferred_element_type=F32)

    span = group * blk
    last_off = ka_ref.shape[0] - span

    def logits_to(buf, h, g):
        off = pl.multiple_of(jnp.minimum(g * span, last_off), span)
        k_grp = ka_ref[pl.ds(off, span), h * aug:(h + 1) * aug]
        s = jnp.dot(k_grp, qa_ref[h * aug:(h + 1) * aug, :], preferred_element_type=F32)
        s_ref[buf, h] = s
        return jnp.max(s, axis=0, keepdims=True)

    def update(buf, h, g, state, mx):
        m, l, acc = state
        off = pl.multiple_of(g * span, span)
        m_new = jnp.maximum(m, mx)
        p = jnp.exp2(s_ref[buf, h] - m_new)
        corr = jnp.exp2(m - m_new)
        l_new = corr * l + jnp.sum(p, axis=0, keepdims=True)
        v_grp = vt_ref[h * HEAD_DIM:(h + 1) * HEAD_DIM, pl.ds(off, span)]
        return m_new, l_new, corr * acc + jnp.dot(v_grp, p.astype(BF16), preferred_element_type=F32)

    def body(t, carry):
        states, mx_a = list(carry[0]), list(carry[1])
        for h in range(heads):
            mx_b = logits_to(1, h, 2 * t + 1)
            states[h] = update(0, h, 2 * t, states[h], mx_a[h])
            mx_a[h] = mx_b
        for h in range(heads):
            mx_b = mx_a[h]
            mx_a[h] = logits_to(0, h, 2 * t + 2)
            states[h] = update(1, h, 2 * t + 1, states[h], mx_b)
        return tuple(states), tuple(mx_a)

    n_swept = qb0 + tq // blk - 1
    n_trips = (n_swept + 2 * group - 1) // (2 * group)
    first = (tuple(init(h) for h in range(heads)), tuple(logits_to(0, h, 0) for h in range(heads)))
    states, _ = lax.fori_loop(0, n_trips, body, first)
    for h in range(heads):
        m, l, acc = states[h]
        o_ref[:, h * HEAD_DIM:(h + 1) * HEAD_DIM] = (acc / l).T.astype(o_ref.dtype)


def _attention(qa, ka, vt, batch, seq, group=4, heads=2, tq=2 * MOBA_BLOCK):
    t = ka.shape[0]
    w = N_HEADS * HEAD_DIM
    nb = seq // MOBA_BLOCK
    aug = HEAD_DIM + LANES
    assert nb % (2 * group) == 0 and N_HEADS % heads == 0 and seq % tq == 0 and tq % MOBA_BLOCK == 0
    hg = N_HEADS // heads
    nt = seq // tq
    kern = functools.partial(_attn_kernel, group=group, heads=heads)
    return pl.pallas_call(
        kern,
        grid=(batch, hg, nt),
        in_specs=[pl.BlockSpec((heads * aug, tq), lambda b, h, n: (b * hg + h, n)),
                  pl.BlockSpec((seq, heads * aug), lambda b, h, n: (b, h), pipeline_mode=pl.Buffered(1)),
                  pl.BlockSpec((heads * HEAD_DIM, seq), lambda b, h, n: (b * hg + h, 0),
                               pipeline_mode=pl.Buffered(1))],
        out_specs=pl.BlockSpec((tq, heads * HEAD_DIM), lambda b, h, n: (b * nt + n, h)),
        out_shape=jax.ShapeDtypeStruct((t, w), BF16),
        scratch_shapes=[pltpu.VMEM((2, heads, group * MOBA_BLOCK, tq), F32)],
        compiler_params=_params("arbitrary", "arbitrary", "arbitrary"),
        name="moba_attn",
    )(qa, ka, vt)


def _merge_kernel(hc_ref, o_ref, gc0_ref, gc1_ref, ga0_ref, ga1_ref, x_ref, g1_ref, sc2_ref, sh2_ref,
                  wc_ref, bc_ref, wa_ref, wo_ref, lg_ref, lb_ref, x1_ref, u2_ref, *, rc):
    for r0 in range(0, x_ref.shape[0], rc):
        rows = slice(r0, r0 + rc)
        yc = jnp.dot(hc_ref[rows, :], wc_ref[...], preferred_element_type=F32) + bc_ref[...]
        ya = jnp.dot(o_ref[rows, :], wa_ref[...], preferred_element_type=F32)
        gc = jax.nn.sigmoid(jnp.concatenate([gc0_ref[rows, :], gc1_ref[rows, :]], axis=1))
        ga = jax.nn.sigmoid(jnp.concatenate([ga0_ref[rows, :], ga1_ref[rows, :]], axis=1))
        mix = (gc * yc + ga * ya).astype(BF16)
        y = jnp.dot(mix, wo_ref[...], preferred_element_type=F32)
        z = DEEPNORM_ALPHA * x_ref[rows, :] + (1.0 + g1_ref[0]) * y
        x1 = _layer_norm(z, lg_ref[...], lb_ref[...])
        x1_ref[rows, :] = x1
        u2_ref[rows, :] = (x1 * (1.0 + sc2_ref[0]) + sh2_ref[0]).astype(BF16)


def _merge(hc, o, proj, x2d, gate1, scale2, shift2, wc, bc, wa, wo, lg, lb, seq, gcol, tm=256, rc=256):
    t, d = x2d.shape
    ch = hc.shape[1]
    w = o.shape[1]
    tpb = seq // tm
    row = lambda cols, c: pl.BlockSpec((tm, cols), lambda i: (i, c))
    mod = pl.BlockSpec((1, 1, d), lambda i: (i // tpb, 0, 0))
    return pl.pallas_call(
        functools.partial(_merge_kernel, rc=min(rc, tm)),
        grid=(t // tm,),
        in_specs=[row(ch, 0), row(w, 0),
                  row(d // 2, gcol), row(d // 2, gcol + 1), row(d // 2, gcol + 2), row(d // 2, gcol + 3),
                  row(d, 0), mod, mod, mod,
                  _const_spec(wc.shape), _const_spec((1, d)), _const_spec(wa.shape), _const_spec(wo.shape),
                  _const_spec((1, d)), _const_spec((1, d))],
        out_specs=[row(d, 0), row(d, 0)],
        out_shape=[jax.ShapeDtypeStruct((t, d), F32), jax.ShapeDtypeStruct((t, d), BF16)],
        compiler_params=_params("arbitrary"),
        name="merge_ln1",
    )(hc, o, proj, proj, proj, proj, x2d, gate1, scale2, shift2, wc, bc, wa, wo, lg, lb)


def _ffn_up_kernel(u_ref, wa_ref, wv_ref, cw_ref, cb_ref, h_ref, abuf_ref, carry_ref, *, tm, tpb, rc):
    i = pl.program_id(0)
    f = pl.program_id(1)

    first_of_batch = (i % tpb) == 0

    @pl.when(first_of_batch)
    def _():
        abuf_ref[0:FFN_HALO, :] = jnp.zeros((FFN_HALO, abuf_ref.shape[1]), F32)

    @pl.when(jnp.logical_not(first_of_batch))
    def _():
        abuf_ref[0:FFN_HALO, :] = carry_ref[f]

    for r0 in range(0, tm, rc):
        u = u_ref[r0:r0 + rc, :]
        a = jnp.dot(u, wa_ref[...], preferred_element_type=F32)
        val = jnp.dot(u, wv_ref[...], preferred_element_type=F32)
        abuf_ref[FFN_HALO + r0:FFN_HALO + r0 + rc, :] = a
        conv = a * cw_ref[FFN_CONV - 1:FFN_CONV, :] + cb_ref[...]
        for j in range(FFN_CONV - 1):
            start = FFN_HALO + r0 - (FFN_CONV - 1 - j)
            conv = conv + abuf_ref[start:start + rc, :] * cw_ref[j:j + 1, :]
        h_ref[r0:r0 + rc, :] = (_silu(conv) * val).astype(h_ref.dtype)
    carry_ref[f] = abuf_ref[tm:tm + FFN_HALO, :]


def _ffn_up(u2, w_up, cw, cb, dff, seq, tm, tf, rc=256):
    t, d = u2.shape
    nf = dff // tf
    tpb = seq // tm
    kern = functools.partial(_ffn_up_kernel, tm=tm, tpb=tpb, rc=min(rc, tm))
    return pl.pallas_call(
        kern,
        grid=(t // tm, nf),
        in_specs=[pl.BlockSpec((tm, d), lambda i, f: (i, 0)),
                  pl.BlockSpec((d, tf), lambda i, f: (0, f)),
                  pl.BlockSpec((d, tf), lambda i, f: (0, nf + f)),
                  pl.BlockSpec((FFN_CONV, tf), lambda i, f: (0, f)),
                  pl.BlockSpec((1, tf), lambda i, f: (0, f))],
        out_specs=pl.BlockSpec((tm, tf), lambda i, f: (i, f)),
        out_shape=jax.ShapeDtypeStruct((t, dff), BF16),
        scratch_shapes=[pltpu.VMEM((tm + FFN_HALO, tf), F32),
                        pltpu.VMEM((nf, FFN_HALO, tf), F32)],
        compiler_params=_params("arbitrary", "arbitrary"),
        name="convffn_up",
    )(u2, w_up, w_up, cw, cb)


def _ffn_down_kernel(h_ref, wd_ref, x1_ref, g2_ref, lg_ref, lb_ref, o_ref, *, rc):
    for r0 in range(0, h_ref.shape[0], rc):
        y = jnp.dot(h_ref[r0:r0 + rc, :], wd_ref[...], preferred_element_type=F32)
        z = DEEPNORM_ALPHA * x1_ref[r0:r0 + rc, :] + (1.0 + g2_ref[0]) * y
        o_ref[r0:r0 + rc, :] = _layer_norm(z, lg_ref[...], lb_ref[...])


def _ffn_down(hid, w_down, x1, gate2, lg, lb, seq, tm, rc=256):
    t, dff = hid.shape
    d = w_down.shape[1]
    tpb = seq // tm
    return pl.pallas_call(
        functools.partial(_ffn_down_kernel, rc=min(rc, tm)),
        grid=(t // tm,),
        in_specs=[pl.BlockSpec((tm, dff), lambda i: (i, 0)),
                  _const_spec(w_down.shape),
                  pl.BlockSpec((tm, d), lambda i: (i, 0)),
                  pl.BlockSpec((1, 1, d), lambda i: (i // tpb, 0, 0)),
                  _const_spec((1, d)), _const_spec((1, d))],
        out_specs=pl.BlockSpec((tm, d), lambda i: (i, 0)),
        out_shape=jax.ShapeDtypeStruct((t, d), F32),
        compiler_params=_params("arbitrary"),
        name="ffn_down_ln2",
    )(hid, w_down, x1, gate2, lg, lb)


def _rope_tables(seq):
    half = ROT_DIM // 2
    pos = jnp.arange(seq, dtype=F32)
    inv_freq = ROPE_THETA ** (-jnp.arange(0, ROT_DIM, 2, dtype=F32) / ROT_DIM)
    ang = pos[:, None] * inv_freq[None, :]
    cos, sin = jnp.cos(ang), jnp.sin(ang)
    ones = jnp.ones((seq, HEAD_DIM - ROT_DIM), F32)
    zeros = jnp.zeros((seq, HEAD_DIM - ROT_DIM), F32)
    zh = jnp.zeros((seq, half), F32)
    tab_c = jnp.concatenate([cos, cos, ones], axis=1)
    tab_a = jnp.concatenate([-sin, zh, zeros], axis=1)
    tab_b = jnp.concatenate([zh, sin, zeros], axis=1)
    tab_s = jnp.concatenate([-sin, sin, zeros], axis=1)
    return (tab_c.T, tab_s.T), (tab_c, tab_a, tab_b)


def kernel(x, c, w_ada, b_ada, w_in, conv_dw_w, conv_dw_b, conv_ln_g, conv_ln_b, w_conv_out, b_conv_out,
           w_attn_out, w_out, ln1_g, ln1_b, w_up, ffn_dw_w, ffn_dw_b, w_down, ln2_g, ln2_b):
    batch, seq, d = x.shape
    ch = conv_dw_w.shape[-1]
    width = N_HEADS * HEAD_DIM
    assert seq % MOBA_BLOCK == 0 and w_ada.shape[0] == DEPTH
    assert 2 * ch % width == 0 and (2 * ch + 3 * width) % (d // 2) == 0
    x2d = x.reshape(batch * seq, d)
    row = lambda v: v.reshape(1, -1)

    c_pad = jnp.pad(c, ((0, SUBLANES - batch), (0, 0)))
    mod = _ada(c_pad, w_ada[0], row(b_ada[0]))[:batch]
    shift1, scale1, gate1, shift2, scale2, gate2 = [m.reshape(batch, 1, d) for m in jnp.split(mod, 6, axis=-1)]

    proj = _inproj(x2d, scale1, shift1, w_in[0].astype(BF16), seq, tm=min(1024, seq), tn=1024)

    hc = _conv_branch(proj, conv_dw_w[0], row(conv_dw_b[0]), row(conv_ln_g[0]), row(conv_ln_b[0]),
                      batch, seq, ch, tm=min(256, seq))

    tabs_t, tabs = _rope_tables(seq)
    qa, ka, vt = _prep(proj, tabs_t, tabs, batch, seq, qcol=2 * ch // width)
    o = _attention(qa, ka, vt, batch, seq, group=min(4, seq // MOBA_BLOCK // 2))

    x1, u2 = _merge(hc, o, proj, x2d, gate1, scale2, shift2,
                    w_conv_out[0].astype(BF16), row(b_conv_out[0]), w_attn_out[0].astype(BF16),
                    w_out[0].astype(BF16), row(ln1_g[0]), row(ln1_b[0]), seq,
                    gcol=(2 * ch + 3 * width) // (d // 2))

    hid = _ffn_up(u2, w_up[0].astype(BF16), ffn_dw_w[0], row(ffn_dw_b[0]), w_down.shape[1], seq,
                  tm=min(1024, seq), tf=512)
    out = _ffn_down(hid, w_down[0].astype(BF16), x1, gate2, row(ln2_g[0]), row(ln2_b[0]), seq, tm=min(512, seq))
    return out.reshape(batch, seq, d)
```

```python
import functools

import jax
import jax.numpy as jnp
from jax import lax
from jax.experimental import pallas as pl
from jax.experimental.pallas import tpu as pltpu

F32 = jnp.float32
BF16 = jnp.bfloat16

N_HEADS = 8
HEAD_DIM = 128
ROT_DIM = HEAD_DIM // 4
ROPE_THETA = 500000.0
MOBA_BLOCK = 256
MOBA_TOPK = 3
CONV_KERNEL = 31
FFN_CONV = 3
LN_EPS = 1e-5
DEPTH = 1
DEEPNORM_ALPHA = (2.0 * DEPTH) ** 0.25

LANES = 128
SUBLANES = 8
MASK_BIAS = -(2.0 ** 30)
LOG2_E = 1.4426950408889634
CONV_HALO = 32
FFN_HALO = SUBLANES
VMEM_LIMIT = 56 * 1024 * 1024


def _params(*sem):
    return pltpu.CompilerParams(dimension_semantics=sem, vmem_limit_bytes=VMEM_LIMIT)


def _const_spec(shape):
    nd = len(shape)
    return pl.BlockSpec(shape, lambda *_: (0,) * nd, pipeline_mode=pl.Buffered(1))


def _layer_norm(z, g, b):
    mu = jnp.mean(z, axis=-1, keepdims=True)
    zc = z - mu
    var = jnp.mean(zc * zc, axis=-1, keepdims=True)
    return zc * lax.rsqrt(var + LN_EPS) * g + b


def _silu(v):
    return v * jax.nn.sigmoid(v)


def _ada_kernel(c_ref, w_ref, b_ref, o_ref):
    ca = _silu(c_ref[...]).astype(BF16)
    o_ref[...] = jnp.dot(ca, w_ref[...].astype(BF16), preferred_element_type=F32) + b_ref[...]


def _ada(c_pad, w_ada, b_ada, tn=1024):
    rows, d = c_pad.shape
    n = w_ada.shape[1]
    return pl.pallas_call(
        _ada_kernel,
        grid=(n // tn,),
        in_specs=[pl.BlockSpec((rows, d), lambda j: (0, 0)),
                  pl.BlockSpec((d, tn), lambda j: (0, j)),
                  pl.BlockSpec((1, tn), lambda j: (0, j))],
        out_specs=pl.BlockSpec((rows, tn), lambda j: (0, j)),
        out_shape=jax.ShapeDtypeStruct((rows, n), F32),
        compiler_params=_params("arbitrary"),
        name="ada",
    )(c_pad, w_ada, b_ada)


def _inproj_kernel(x_ref, sc_ref, sh_ref, w_ref, o_ref, u_ref):
    @pl.when(pl.program_id(1) == 0)
    def _():
        u_ref[...] = (x_ref[...] * (1.0 + sc_ref[0]) + sh_ref[0]).astype(BF16)

    o_ref[...] = jnp.dot(u_ref[...], w_ref[...], preferred_element_type=F32)


def _inproj(x2d, scale, shift, w, seq, tm, tn):
    t, d = x2d.shape
    n = w.shape[1]
    tpb = seq // tm
    return pl.pallas_call(
        _inproj_kernel,
        grid=(t // tm, n // tn),
        in_specs=[pl.BlockSpec((tm, d), lambda i, j: (i, 0)),
                  pl.BlockSpec((1, 1, d), lambda i, j: (i // tpb, 0, 0)),
                  pl.BlockSpec((1, 1, d), lambda i, j: (i // tpb, 0, 0)),
                  pl.BlockSpec((d, tn), lambda i, j: (0, j))],
        out_specs=pl.BlockSpec((tm, tn), lambda i, j: (i, j)),
        out_shape=jax.ShapeDtypeStruct((t, n), F32),
        scratch_shapes=[pltpu.VMEM((tm, d), BF16)],
        compiler_params=_params("arbitrary", "arbitrary"),
        name="inproj",
    )(x2d, scale, shift, w)


def _conv_kernel(al_ref, ag_ref, w_ref, b_ref, g_ref, be_ref, o_ref, win_ref, wb_ref, y_ref, *,
                 tm, rows, lanes, ln_rows):
    i = pl.program_id(1)

    ch = win_ref.shape[2]

    @pl.when(jnp.logical_and(pl.program_id(0) == 0, i == 0))
    def _():
        for j in range(CONV_KERNEL):
            wb_ref[j] = jnp.broadcast_to(w_ref[j:j + 1, :], (SUBLANES, ch))
        wb_ref[CONV_KERNEL] = jnp.broadcast_to(b_ref[...], (SUBLANES, ch))

    @pl.when(i == 0)
    def _():
        win_ref[0, 0:CONV_HALO, :] = jnp.zeros((CONV_HALO, ch), F32)

    @pl.when(i > 0)
    def _():
        win_ref[0, 0:CONV_HALO, :] = win_ref[0, tm:tm + CONV_HALO, :]

    win_ref[0, CONV_HALO:CONV_HALO + tm, :] = al_ref[...] * jax.nn.sigmoid(ag_ref[...])

    n_shift = tm + CONV_HALO - SUBLANES
    for s in range(1, SUBLANES):
        win_ref[s, 0:n_shift, :] = win_ref[0, s:s + n_shift, :]

    first = CONV_HALO - (CONV_KERNEL - 1)

    def body(s, carry):
        r0 = pl.multiple_of(s * rows, rows)
        for c0 in range(0, ch, lanes):
            cs = slice(c0, c0 + lanes)
            accs = [wb_ref[CONV_KERNEL, :, cs]] * (rows // SUBLANES)
            for j in range(CONV_KERNEL):
                tiles, rem = divmod(first + j, SUBLANES)
                wj = wb_ref[j, :, cs]
                for k in range(rows // SUBLANES):
                    start = pl.multiple_of(r0 + (tiles + k) * SUBLANES, SUBLANES)
                    accs[k] = accs[k] + win_ref[rem, pl.ds(start, SUBLANES), cs] * wj
            y_ref[pl.ds(r0, rows), cs] = jnp.concatenate(accs, axis=0)
        for rr in range(0, rows, ln_rows):
            y = y_ref[pl.ds(pl.multiple_of(r0 + rr, ln_rows), ln_rows), :]
            y = _silu(_layer_norm(y, g_ref[...], be_ref[...]))
            o_ref[pl.ds(pl.multiple_of(r0 + rr, ln_rows), ln_rows), :] = y.astype(o_ref.dtype)
        return carry

    lax.fori_loop(0, tm // rows, body, 0)


def _conv_branch(proj, dw_w, dw_b, ln_g, ln_b, batch, seq, ch, tm=256, rows=128, lanes=256, ln_rows=32):
    t = proj.shape[0]
    tpb = seq // tm
    kern = functools.partial(_conv_kernel, tm=tm, rows=rows, lanes=lanes, ln_rows=ln_rows)
    return pl.pallas_call(
        kern,
        grid=(batch, tpb),
        in_specs=[pl.BlockSpec((tm, ch), lambda b, i: (b * tpb + i, 0)),
                  pl.BlockSpec((tm, ch), lambda b, i: (b * tpb + i, 1)),
                  _const_spec(dw_w.shape), _const_spec((1, ch)), _const_spec((1, ch)), _const_spec((1, ch))],
        out_specs=pl.BlockSpec((tm, ch), lambda b, i: (b * tpb + i, 0)),
        out_shape=jax.ShapeDtypeStruct((t, ch), BF16),
        scratch_shapes=[pltpu.VMEM((SUBLANES, tm + CONV_HALO, ch), F32),
                        pltpu.VMEM((CONV_KERNEL + 1, SUBLANES, ch), F32),
                        pltpu.VMEM((tm, ch), F32)],
        compiler_params=_params("arbitrary", "arbitrary"),
        name="conv_branch",
    )(proj, proj, dw_w, dw_b, ln_g, ln_b)


def _prep_kernel(q_ref, k_ref, v_ref, ct_ref, st_ref, c_ref, a_ref, b_ref, qa_ref, ka_ref, vt_ref, km_ref):
    n = pl.program_id(1)
    blk = q_ref.shape[0]
    half = ROT_DIM // 2

    @pl.when(n == 0)
    def _():
        km_ref[...] = jnp.zeros(km_ref.shape, F32)

    cc, aa, bb = c_ref[...], a_ref[...], b_ref[...]
    ct, st = ct_ref[...], st_ref[...]

    def rope(xh):
        return xh * cc + pltpu.roll(xh, LANES - half, 1) * aa + pltpu.roll(xh, half, 1) * bb

    def rope_t(xt):
        partner = jnp.concatenate([xt[half:ROT_DIM], xt[:half], xt[ROT_DIM:]], axis=0)
        return xt * ct + partner * st

    lane = lax.broadcasted_iota(jnp.int32, (blk, LANES), 1)
    onehot_n = jnp.where(lane == n, 1.0, 0.0).astype(BF16)
    ids = lax.broadcasted_iota(jnp.int32, (LANES, blk), 0).astype(F32)
    n_past = jnp.full((LANES, blk), n.astype(F32))
    km_row = lax.broadcasted_iota(jnp.int32, (km_ref.shape[0], HEAD_DIM), 0)
    scale = HEAD_DIM ** -0.5 * LOG2_E
    aug = HEAD_DIM + LANES

    for h in range(N_HEADS):
        sl = slice(h * HEAD_DIM, (h + 1) * HEAD_DIM)
        qt = rope_t(q_ref[:, sl].T)
        kr = rope(k_ref[:, sl])
        km = km_ref[:, sl]

        gate = jnp.dot(km.astype(BF16), qt.astype(BF16), preferred_element_type=F32)
        g = jnp.where(ids < n_past, gate, -jnp.inf)
        sel = jnp.zeros(gate.shape, jnp.bool_)
        for r in range(MOBA_TOPK):
            mx = jnp.max(g, axis=0, keepdims=True)
            first = jnp.min(jnp.where(g == mx, ids, float(LANES)), axis=0, keepdims=True)
            pick = ids == first
            sel = jnp.logical_or(sel, jnp.logical_and(pick, n_past > float(r)))
            g = jnp.where(pick, -jnp.inf, g)

        qa_ref[h * aug:h * aug + HEAD_DIM, :] = (qt * scale).astype(BF16)
        qa_ref[h * aug + HEAD_DIM:(h + 1) * aug, :] = jnp.where(sel, 0.0, MASK_BIAS).astype(BF16)
        ka_ref[:, h * aug:h * aug + HEAD_DIM] = kr.astype(BF16)
        ka_ref[:, h * aug + HEAD_DIM:(h + 1) * aug] = onehot_n
        km_ref[:, sl] = jnp.where(km_row == n, jnp.mean(kr, axis=0, keepdims=True), km)
    vt_ref[...] = v_ref[...].T.astype(BF16)


def _prep(proj, tabs_t, tabs, batch, seq, qcol):
    t = proj.shape[0]
    w = N_HEADS * HEAD_DIM
    nb = seq // MOBA_BLOCK
    assert nb <= LANES
    blk = lambda c: pl.BlockSpec((MOBA_BLOCK, w), lambda b, n: (b * nb + n, c))
    tab = pl.BlockSpec((MOBA_BLOCK, HEAD_DIM), lambda b, n: (n, 0))
    tab_t = pl.BlockSpec((HEAD_DIM, MOBA_BLOCK), lambda b, n: (0, n))
    return pl.pallas_call(
        _prep_kernel,
        grid=(batch, nb),
        in_specs=[blk(qcol), blk(qcol + 1), blk(qcol + 2), tab_t, tab_t, tab, tab, tab],
        out_specs=[pl.BlockSpec((2 * w, MOBA_BLOCK), lambda b, n: (b, n)),
                   pl.BlockSpec((MOBA_BLOCK, 2 * w), lambda b, n: (b * nb + n, 0)),
                   pl.BlockSpec((w, MOBA_BLOCK), lambda b, n: (b, n))],
        out_shape=[jax.ShapeDtypeStruct((batch * 2 * w, seq), BF16), jax.ShapeDtypeStruct((t, 2 * w), BF16),
                   jax.ShapeDtypeStruct((batch * w, seq), BF16)],
        scratch_shapes=[pltpu.VMEM((LANES, w), F32)],
        compiler_params=_params("arbitrary", "arbitrary"),
        name="rope_prep",
    )(proj, proj, proj, *tabs_t, *tabs)


def _attn_kernel(qa_ref, ka_ref, vt_ref, o_ref, s_ref, *, group, heads):
    blk = MOBA_BLOCK
    tq = o_ref.shape[0]
    qb0 = pl.program_id(2) * (tq // blk)
    aug = HEAD_DIM + LANES
    start = pl.multiple_of(qb0 * blk, tq)
    key = lax.broadcasted_iota(jnp.int32, (tq, tq), 0)
    qry = lax.broadcasted_iota(jnp.int32, (tq, tq), 1)
    sh = blk.bit_length() - 1
    own = jnp.logical_and(key <= qry, lax.shift_right_logical(key, sh) == lax.shift_right_logical(qry, sh))

    def init(h):
        q_t = qa_ref[h * aug:h * aug + HEAD_DIM, :]
        k_own = ka_ref[pl.ds(start, tq), h * aug:h * aug + HEAD_DIM]
        s = jnp.dot(k_own, q_t, preferred_element_type=F32)
        s = jnp.where(own, s, -jnp.inf)
        m = jnp.max(s, axis=0, keepdims=True)
        p = jnp.exp2(s - m)
        l = jnp.sum(p, axis=0, keepdims=True)
        v_own = vt_ref[h * HEAD_DIM:(h + 1) * HEAD_DIM, pl.ds(start, tq)]
        return m, l, jnp.dot(v_own, p.astype(BF16), preferred_element_type=F32)

    span = group * blk
    last_off = ka_ref.shape[0] - span

    def logits_to(buf, h, g):
        off = pl.multiple_of(jnp.minimum(g * span, last_off), span)
        k_grp = ka_ref[pl.ds(off, span), h * aug:(h + 1) * aug]
        s = jnp.dot(k_grp, qa_ref[h * aug:(h + 1) * aug, :], preferred_element_type=F32)
        s_ref[buf, h] = s
        return jnp.max(s, axis=0, keepdims=True)

    def update(buf, h, g, state, mx):
        m, l, acc = state
        off = pl.multiple_of(g * span, span)
        m_new = jnp.maximum(m, mx)
        p = jnp.exp2(s_ref[buf, h] - m_new)
        corr = jnp.exp2(m - m_new)
        l_new = corr * l + jnp.sum(p, axis=0, keepdims=True)
        v_grp = vt_ref[h * HEAD_DIM:(h + 1) * HEAD_DIM, pl.ds(off, span)]
        return m_new, l_new, corr * acc + jnp.dot(v_grp, p.astype(BF16), preferred_element_type=F32)

    def body(t, carry):
        states, mx_a = list(carry[0]), list(carry[1])
        for half in range(2):
            g0 = 4 * t + 2 * half
            for h in range(heads):
                mx_b = logits_to(1, h, g0 + 1)
                states[h] = update(0, h, g0, states[h], mx_a[h])
                mx_a[h] = mx_b
            for h in range(heads):
                mx_b = mx_a[h]
                mx_a[h] = logits_to(0, h, g0 + 2)
                states[h] = update(1, h, g0 + 1, states[h], mx_b)
        return tuple(states), tuple(mx_a)

    n_swept = qb0 + tq // blk - 1
    n_trips = (n_swept + 4 * group - 1) // (4 * group)
    first = (tuple(init(h) for h in range(heads)), tuple(logits_to(0, h, 0) for h in range(heads)))
    states, _ = lax.fori_loop(0, n_trips, body, first)
    for h in range(heads):
        m, l, acc = states[h]
        o_ref[:, h * HEAD_DIM:(h + 1) * HEAD_DIM] = (acc / l).T.astype(o_ref.dtype)


def _attention(qa, ka, vt, batch, seq, group=4, heads=2, tq=2 * MOBA_BLOCK):
    t = ka.shape[0]
    w = N_HEADS * HEAD_DIM
    nb = seq // MOBA_BLOCK
    aug = HEAD_DIM + LANES
    assert nb % (4 * group) == 0 and N_HEADS % heads == 0 and seq % tq == 0 and tq % MOBA_BLOCK == 0
    hg = N_HEADS // heads
    nt = seq // tq
    kern = functools.partial(_attn_kernel, group=group, heads=heads)
    return pl.pallas_call(
        kern,
        grid=(batch, hg, nt),
        in_specs=[pl.BlockSpec((heads * aug, tq), lambda b, h, n: (b * hg + h, n)),
                  pl.BlockSpec((seq, heads * aug), lambda b, h, n: (b, h), pipeline_mode=pl.Buffered(1)),
                  pl.BlockSpec((heads * HEAD_DIM, seq), lambda b, h, n: (b * hg + h, 0),
                               pipeline_mode=pl.Buffered(1))],
        out_specs=pl.BlockSpec((tq, heads * HEAD_DIM), lambda b, h, n: (b * nt + n, h)),
        out_shape=jax.ShapeDtypeStruct((t, w), BF16),
        scratch_shapes=[pltpu.VMEM((2, heads, group * MOBA_BLOCK, tq), F32)],
        compiler_params=_params("arbitrary", "arbitrary", "arbitrary"),
        name="moba_attn",
    )(qa, ka, vt)


def _merge_kernel(hc_ref, o_ref, gc0_ref, gc1_ref, ga0_ref, ga1_ref, x_ref, g1_ref, sc2_ref, sh2_ref,
                  wc_ref, bc_ref, wa_ref, wo_ref, lg_ref, lb_ref, x1_ref, u2_ref, *, rc):
    for r0 in range(0, x_ref.shape[0], rc):
        rows = slice(r0, r0 + rc)
        yc = jnp.dot(hc_ref[rows, :], wc_ref[...], preferred_element_type=F32) + bc_ref[...]
        ya = jnp.dot(o_ref[rows, :], wa_ref[...], preferred_element_type=F32)
        gc = jax.nn.sigmoid(jnp.concatenate([gc0_ref[rows, :], gc1_ref[rows, :]], axis=1))
        ga = jax.nn.sigmoid(jnp.concatenate([ga0_ref[rows, :], ga1_ref[rows, :]], axis=1))
        mix = (gc * yc + ga * ya).astype(BF16)
        y = jnp.dot(mix, wo_ref[...], preferred_element_type=F32)
        z = DEEPNORM_ALPHA * x_ref[rows, :] + (1.0 + g1_ref[0]) * y
        x1 = _layer_norm(z, lg_ref[...], lb_ref[...])
        x1_ref[rows, :] = x1
        u2_ref[rows, :] = (x1 * (1.0 + sc2_ref[0]) + sh2_ref[0]).astype(BF16)


def _merge(hc, o, proj, x2d, gate1, scale2, shift2, wc, bc, wa, wo, lg, lb, seq, gcol, tm=256, rc=256):
    t, d = x2d.shape
    ch = hc.shape[1]
    w = o.shape[1]
    tpb = seq // tm
    row = lambda cols, c: pl.BlockSpec((tm, cols), lambda i: (i, c))
    mod = pl.BlockSpec((1, 1, d), lambda i: (i // tpb, 0, 0))
    return pl.pallas_call(
        functools.partial(_merge_kernel, rc=min(rc, tm)),
        grid=(t // tm,),
        in_specs=[row(ch, 0), row(w, 0),
                  row(d // 2, gcol), row(d // 2, gcol + 1), row(d // 2, gcol + 2), row(d // 2, gcol + 3),
                  row(d, 0), mod, mod, mod,
                  _const_spec(wc.shape), _const_spec((1, d)), _const_spec(wa.shape), _const_spec(wo.shape),
                  _const_spec((1, d)), _const_spec((1, d))],
        out_specs=[row(d, 0), row(d, 0)],
        out_shape=[jax.ShapeDtypeStruct((t, d), F32), jax.ShapeDtypeStruct((t, d), BF16)],
        compiler_params=_params("arbitrary"),
        name="merge_ln1",
    )(hc, o, proj, proj, proj, proj, x2d, gate1, scale2, shift2, wc, bc, wa, wo, lg, lb)


def _ffn_up_kernel(u_ref, wa_ref, wv_ref, cw_ref, cb_ref, h_ref, abuf_ref, carry_ref, *, tm, tpb, rc):
    i = pl.program_id(0)
    f = pl.program_id(1)

    first_of_batch = (i % tpb) == 0

    @pl.when(first_of_batch)
    def _():
        abuf_ref[0:FFN_HALO, :] = jnp.zeros((FFN_HALO, abuf_ref.shape[1]), F32)

    @pl.when(jnp.logical_not(first_of_batch))
    def _():
        abuf_ref[0:FFN_HALO, :] = carry_ref[f]

    for r0 in range(0, tm, rc):
        u = u_ref[r0:r0 + rc, :]
        a = jnp.dot(u, wa_ref[...], preferred_element_type=F32)
        val = jnp.dot(u, wv_ref[...], preferred_element_type=F32)
        abuf_ref[FFN_HALO + r0:FFN_HALO + r0 + rc, :] = a
        conv = a * cw_ref[FFN_CONV - 1:FFN_CONV, :] + cb_ref[...]
        for j in range(FFN_CONV - 1):
            start = FFN_HALO + r0 - (FFN_CONV - 1 - j)
            conv = conv + abuf_ref[start:start + rc, :] * cw_ref[j:j + 1, :]
        h_ref[r0:r0 + rc, :] = (_silu(conv) * val).astype(h_ref.dtype)
    carry_ref[f] = abuf_ref[tm:tm + FFN_HALO, :]


def _ffn_up(u2, w_up, cw, cb, dff, seq, tm, tf, rc=256):
    t, d = u2.shape
    nf = dff // tf
    tpb = seq // tm
    kern = functools.partial(_ffn_up_kernel, tm=tm, tpb=tpb, rc=min(rc, tm))
    return pl.pallas_call(
        kern,
        grid=(t // tm, nf),
        in_specs=[pl.BlockSpec((tm, d), lambda i, f: (i, 0)),
                  pl.BlockSpec((d, tf), lambda i, f: (0, f)),
                  pl.BlockSpec((d, tf), lambda i, f: (0, nf + f)),
                  pl.BlockSpec((FFN_CONV, tf), lambda i, f: (0, f)),
                  pl.BlockSpec((1, tf), lambda i, f: (0, f))],
        out_specs=pl.BlockSpec((tm, tf), lambda i, f: (i, f)),
        out_shape=jax.ShapeDtypeStruct((t, dff), BF16),
        scratch_shapes=[pltpu.VMEM((tm + FFN_HALO, tf), F32),
                        pltpu.VMEM((nf, FFN_HALO, tf), F32)],
        compiler_params=_params("arbitrary", "arbitrary"),
        name="convffn_up",
    )(u2, w_up, w_up, cw, cb)


def _ffn_down_kernel(h_ref, wd_ref, x1_ref, g2_ref, lg_ref, lb_ref, o_ref, *, rc):
    for r0 in range(0, h_ref.shape[0], rc):
        y = jnp.dot(h_ref[r0:r0 + rc, :], wd_ref[...], preferred_element_type=F32)
        z = DEEPNORM_ALPHA * x1_ref[r0:r0 + rc, :] + (1.0 + g2_ref[0]) * y
        o_ref[r0:r0 + rc, :] = _layer_norm(z, lg_ref[...], lb_ref[...])


def _ffn_down(hid, w_down, x1, gate2, lg, lb, seq, tm, rc=256):
    t, dff = hid.shape
    d = w_down.shape[1]
    tpb = seq // tm
    return pl.pallas_call(
        functools.partial(_ffn_down_kernel, rc=min(rc, tm)),
        grid=(t // tm,),
        in_specs=[pl.BlockSpec((tm, dff), lambda i: (i, 0)),
                  _const_spec(w_down.shape),
                  pl.BlockSpec((tm, d), lambda i: (i, 0)),
                  pl.BlockSpec((1, 1, d), lambda i: (i // tpb, 0, 0)),
                  _const_spec((1, d)), _const_spec((1, d))],
        out_specs=pl.BlockSpec((tm, d), lambda i: (i, 0)),
        out_shape=jax.ShapeDtypeStruct((t, d), F32),
        compiler_params=_params("arbitrary"),
        name="ffn_down_ln2",
    )(hid, w_down, x1, gate2, lg, lb)


def _rope_tables(seq):
    half = ROT_DIM // 2
    pos = jnp.arange(seq, dtype=F32)
    inv_freq = ROPE_THETA ** (-jnp.arange(0, ROT_DIM, 2, dtype=F32) / ROT_DIM)
    ang = pos[:, None] * inv_freq[None, :]
    cos, sin = jnp.cos(ang), jnp.sin(ang)
    ones = jnp.ones((seq, HEAD_DIM - ROT_DIM), F32)
    zeros = jnp.zeros((seq, HEAD_DIM - ROT_DIM), F32)
    zh = jnp.zeros((seq, half), F32)
    tab_c = jnp.concatenate([cos, cos, ones], axis=1)
    tab_a = jnp.concatenate([-sin, zh, zeros], axis=1)
    tab_b = jnp.concatenate([zh, sin, zeros], axis=1)
    tab_s = jnp.concatenate([-sin, sin, zeros], axis=1)
    return (tab_c.T, tab_s.T), (tab_c, tab_a, tab_b)


def kernel(x, c, w_ada, b_ada, w_in, conv_dw_w, conv_dw_b, conv_ln_g, conv_ln_b, w_conv_out, b_conv_out,
           w_attn_out, w_out, ln1_g, ln1_b, w_up, ffn_dw_w, ffn_dw_b, w_down, ln2_g, ln2_b):
    batch, seq, d = x.shape
    ch = conv_dw_w.shape[-1]
    width = N_HEADS * HEAD_DIM
    assert seq % MOBA_BLOCK == 0 and w_ada.shape[0] == DEPTH
    assert 2 * ch % width == 0 and (2 * ch + 3 * width) % (d // 2) == 0
    x2d = x.reshape(batch * seq, d)
    row = lambda v: v.reshape(1, -1)

    c_pad = jnp.pad(c, ((0, SUBLANES - batch), (0, 0)))
    mod = _ada(c_pad, w_ada[0], row(b_ada[0]))[:batch]
    shift1, scale1, gate1, shift2, scale2, gate2 = [m.reshape(batch, 1, d) for m in jnp.split(mod, 6, axis=-1)]

    proj = _inproj(x2d, scale1, shift1, w_in[0].astype(BF16), seq, tm=min(1024, seq), tn=1024)

    hc = _conv_branch(proj, conv_dw_w[0], row(conv_dw_b[0]), row(conv_ln_g[0]), row(conv_ln_b[0]),
                      batch, seq, ch, tm=min(256, seq))

    tabs_t, tabs = _rope_tables(seq)
    qa, ka, vt = _prep(proj, tabs_t, tabs, batch, seq, qcol=2 * ch // width)
    o = _attention(qa, ka, vt, batch, seq, group=2)

    x1, u2 = _merge(hc, o, proj, x2d, gate1, scale2, shift2,
                    w_conv_out[0].astype(BF16), row(b_conv_out[0]), w_attn_out[0].astype(BF16),
                    w_out[0].astype(BF16), row(ln1_g[0]), row(ln1_b[0]), seq,
                    gcol=(2 * ch + 3 * width) // (d // 2))

    hid = _ffn_up(u2, w_up[0].astype(BF16), ffn_dw_w[0], row(ffn_dw_b[0]), w_down.shape[1], seq,
                  tm=min(1024, seq), tf=512)
    out = _ffn_down(hid, w_down[0].astype(BF16), x1, gate2, row(ln2_g[0]), row(ln2_b[0]), seq, tm=min(512, seq))
    return out.reshape(batch, seq, d)
```
